```python
import math
import jax, jax.numpy as jnp
from jax import lax
import numpy as np

D_MODEL = 1024
BATCH = 32
SEQ = 256
DEPTH = 4
DEC_BATCH = 4
DEC_SEQ = 1024
PAST_LEN = 256

GRID_W = 64
N_EVEN = (DEPTH + 1) // 2
N_ODD = DEPTH // 2
A_HEADS = 4
A_HD = 64
B_HEADS = 8
B_HD = 64
NA_ROWS = 8
NA_COLS = 16
NA_KCB = 2 * NA_COLS
C_HEADS = 4
C_DK = 128
C_DV = 128
C_CHUNK = 64
D_RNN = 512
D_BLOCKS = 8
D_BW = D_RNN // D_BLOCKS
RG_C = 8.0
CONV_W = 4
FF_DENSE = 2816
N_EXPERTS = 8
TOP_K = 2
FF_EXPERT = 2048
ROPE_BASE = 10000.0
EPS = 1e-6
Q_BLOCK = 128
NEG_INF = -1e30
A_W = A_HEADS * 2 * A_HD
B_W = B_HEADS * B_HD
C_W = C_HEADS * C_DK
IN_EVEN = 3 * A_W + 3 * B_W
MIX_EVEN = A_W + B_W
IN_ODD = 4 * C_W + 4 * C_HEADS + 2 * D_RNN
MIX_ODD = C_W + D_RNN
F32 = jnp.float32

kernel_name = 'hybrid_flow_trunk_step'


def rmsnorm(x, g):
    xf = x.astype(F32)
    y = xf * lax.rsqrt(jnp.mean(xf * xf, axis=-1, keepdims=True) + EPS)
    return (y * g.astype(F32)).astype(x.dtype)


def adaln(cvec, w, b):
    return jnp.split(jax.nn.silu(cvec) @ w + b, 6, axis=-1)


def sublayer(x, shift, scale, gate, g_pre, g_post, fn):
    h = rmsnorm(x, g_pre) * (1 + scale[:, None]) + shift[:, None]
    out, aux = fn(h)
    return x + gate[:, None] * rmsnorm(out, g_post), aux


def rope_2d(x):
    S, d = x.shape[1], x.shape[-1]
    half = d // 2
    t = jnp.arange(S)
    pos = jnp.stack([t // GRID_W, t % GRID_W], axis=-1).astype(F32)
    freqs = ROPE_BASE ** (-jnp.arange(0, half, 2, dtype=F32) / half)
    ang = pos[:, :, None] * freqs
    cos = jnp.cos(ang)[None, :, None]
    sin = jnp.sin(ang)[None, :, None]
    xr = x.astype(F32).reshape(x.shape[:-1] + (2, 2, half // 2))
    x1, x2 = xr[..., 0, :], xr[..., 1, :]
    out = jnp.stack([x1 * cos - x2 * sin, x1 * sin + x2 * cos], axis=-2)
    return out.reshape(x.shape).astype(x.dtype)


def map_query_blocks(fn, q):
    B, S = q.shape[:2]
    nb = S // Q_BLOCK
    qb = jnp.moveaxis(q.reshape((B, nb, Q_BLOCK) + q.shape[2:]), 1, 0)
    out = jnp.moveaxis(lax.map(fn, qb), 0, 1)
    return out.reshape((B, S) + out.shape[3:])


def diff_attention(q, k, v, lam_vec, subln_g, lam_init):
    B, S = q.shape[:2]
    lv = lam_vec.astype(F32)
    lam = jnp.exp(jnp.sum(lv[0] * lv[1])) - jnp.exp(jnp.sum(lv[2] * lv[3])) + lam_init

    def block(qb):
        s = jnp.einsum('bqhmd,bkhmd->bhmqk', qb, k).astype(F32) * (A_HD ** -0.5)
        p = jax.nn.softmax(s, axis=-1)
        p = p[:, :, 0] - lam * p[:, :, 1]
        return jnp.einsum('bhqk,bkhe->bqhe', p.astype(v.dtype), v)

    o = map_query_blocks(block, q)
    o = rmsnorm(o, subln_g) * (1.0 - lam_init)
    return o.reshape(B, S, A_W)


def dense_attention(q, k, v):
    B, S = q.shape[:2]

    def block(qb):
        s = jnp.einsum('bqhd,bkhd->bhqk', qb, k).astype(F32) * (B_HD ** -0.5)
        p = jax.nn.softmax(s, axis=-1)
        return jnp.einsum('bhqk,bkhd->bqhd', p.astype(v.dtype), v)

    return map_query_blocks(block, q).reshape(B, S, B_W)


def neighbourhood_attention(q, k, v, k_ctx, v_ctx, rpb):
    B, S, H, d = q.shape
    rows = S // GRID_W
    kr = min(NA_ROWS, rows)
    ncb = GRID_W // NA_COLS
    r = np.arange(rows)
    row_idx = np.clip(r - kr // 2, 0, rows - kr)[:, None] + np.arange(kr)
    jb = np.arange(ncb)
    col_idx = np.clip(jb * NA_COLS - NA_COLS // 2, 0, GRID_W - NA_KCB)[:, None] + np.arange(NA_KCB)
    qcol = jb[:, None] * NA_COLS + np.arange(NA_COLS)
    cstart = np.clip(qcol - NA_COLS // 2, 0, GRID_W - NA_COLS)
    kc = col_idx[:, None, :]
    valid = (kc >= cstart[..., None]) & (kc < cstart[..., None] + NA_COLS)
    dr = row_idx - r[:, None] + NA_ROWS - 1
    dc = np.clip(kc - qcol[..., None] + NA_COLS - 1, 0, 2 * NA_COLS - 2)
    bias = rpb.astype(F32)[:, dr[:, None, None, :, None], dc[None, :, :, None, :]]
    nw = kr * NA_KCB
    bias = jnp.where(valid[None, None, :, :, None, :], bias, NEG_INF).reshape(H, rows, ncb, NA_COLS, nw)
    ri = row_idx[:, None, :, None]
    ci = col_idx[None, :, None, :]
    kw = k.reshape(B, rows, GRID_W, H, d)[:, ri, ci].reshape(B, rows, ncb, nw, H, d)
    vw = v.reshape(B, rows, GRID_W, H, d)[:, ri, ci].reshape(B, rows, ncb, nw, H, d)
    qg = q.reshape(B, rows, ncb, NA_COLS, H, d)
    scale = d ** -0.5
    s_win = jnp.einsum('brjqhd,brjkhd->bhrjqk', qg, kw).astype(F32) * scale + bias[None]
    s_ctx = jnp.einsum('brjqhd,bchd->bhrjqc', qg, k_ctx).astype(F32) * scale
    p = jax.nn.softmax(jnp.concatenate([s_win, s_ctx], axis=-1), axis=-1).astype(v.dtype)
    o = (jnp.einsum('bhrjqk,brjkhd->brjqhd', p[..., :nw], vw)
         + jnp.einsum('bhrjqc,bchd->brjqhd', p[..., nw:], v_ctx))
    return o.reshape(B, S, B_W)


def even_project(h, w_in):
    B, S, _ = h.shape
    qa, ka, va, qb, kb, vb = jnp.split(
        h @ w_in, [A_W, 2 * A_W, 3 * A_W, 3 * A_W + B_W, 3 * A_W + 2 * B_W], axis=-1)
    return (qa.reshape(B, S, A_HEADS, 2, A_HD), ka.reshape(B, S, A_HEADS, 2, A_HD),
            va.reshape(B, S, A_HEADS, 2 * A_HD), qb.reshape(B, S, B_HEADS, B_HD),
            kb.reshape(B, S, B_HEADS, B_HD), vb.reshape(B, S, B_HEADS, B_HD))


def even_mixer_context(h, w_in, w_out, lam_vec, subln_g, lam_init):
    B, S, _ = h.shape
    qa, ka, va, qb, kb, vb = even_project(h, w_in)
    oa = diff_attention(qa, ka, va, lam_vec, subln_g, lam_init)
    ob = dense_attention(qb, kb, vb)
    out = jnp.concatenate([oa, ob], axis=-1) @ w_out
    return out, (ka.reshape(B, S, A_HEADS, 2 * A_HD), va, kb, vb)


def even_mixer_latent(h, ak, av, bk, bv, rpb, w_in, w_out, lam_vec, subln_g, lam_init):
    B, S, _ = h.shape
    qa, ka, va, qb, kb, vb = even_project(h, w_in)
    qa = rope_2d(qa.reshape(B, S, 2 * A_HEADS, A_HD)).reshape(qa.shape)
    ka = rope_2d(ka.reshape(B, S, 2 * A_HEADS, A_HD)).reshape(ka.shape)
    L = ak.shape[1]
    k_all = jnp.concatenate([ka, ak.reshape(B, L, A_HEADS, 2, A_HD).astype(ka.dtype)], axis=1)
    v_all = jnp.concatenate([va, av.astype(va.dtype)], axis=1)
    oa = diff_attention(qa, k_all, v_all, lam_vec, subln_g, lam_init)
    ob = neighbourhood_attention(qb, kb, vb, bk.astype(kb.dtype), bv.astype(vb.dtype), rpb)
    out = jnp.concatenate([oa, ob], axis=-1) @ w_out
    return out, None


def conv_centred(x, w):
    C = x.shape[-1]
    left = (CONV_W - 1) // 2
    right = CONV_W - 1 - left
    return lax.conv_general_dilated(
        x, w.astype(x.dtype)[:, None, :], window_strides=(1,), padding=[(left, right)],
        dimension_numbers=('NWC', 'WIO', 'NWC'), feature_group_count=C)


def mlstm_chunked(q, k, v, i_pre, f_pre, C0, n0, m0):
    B, H, S, _ = q.shape
    L = C_CHUNK
    nc = S // L

    def to_chunks(a):
        return jnp.moveaxis(a.astype(F32).reshape((B, H, nc, L) + a.shape[3:]), 2, 0)

    xs = (to_chunks(q * (C_DK ** -0.5)), to_chunks(k), to_chunks(v), to_chunks(i_pre),
          to_chunks(jax.nn.log_sigmoid(f_pre.astype(F32))))
    causal = jnp.tril(jnp.ones((L, L), dtype=bool))

    def step(carry, chunk):
        C, n, m = carry
        qb, kb, vb, ib, lfb = chunk
        b = jnp.cumsum(lfb, axis=-1)
        dlog = jnp.where(causal, b[..., :, None] - b[..., None, :] + ib[..., None, :], -jnp.inf)
        inter = b + m[..., None]
        m_t = jnp.maximum(inter, jnp.max(dlog, axis=-1))
        w_in = jnp.exp(dlog - m_t[..., None])
        w_st = jnp.exp(inter - m_t)
        s = jnp.einsum('bhtd,bhsd->bhts', qb, kb) * w_in
        num = w_st[..., None] * jnp.einsum('bhtd,bhde->bhte', qb, C) + jnp.einsum('bhts,bhse->bhte', s, vb)
        nq = w_st * jnp.einsum('bhtd,bhd->bht', qb, n) + jnp.sum(s, axis=-1)
        h = num / jnp.maximum(jnp.abs(nq), jnp.exp(-m_t))[..., None]
        b_last = b[..., -1]
        wlog = b_last[..., None] - b + ib
        m_new = jnp.maximum(b_last + m, jnp.max(wlog, axis=-1))
        ws = jnp.exp(wlog - m_new[..., None])
        wc = jnp.exp(b_last + m - m_new)
        C_new = wc[..., None, None] * C + jnp.einsum('bhs,bhsd,bhse->bhde', ws, kb, vb)
        n_new = wc[..., None] * n + jnp.einsum('bhs,bhsd->bhd', ws, kb)
        return (C_new, n_new, m_new), h

    (C, n, m), hs = lax.scan(step, (C0.astype(F32), n0.astype(F32), m0.astype(F32)), xs)
    return jnp.moveaxis(hs, 0, 2).reshape(B, H, S, C_DV), (C, n, m)


def combine_linear(left, right):
    a1, b1 = left
    a2, b2 = right
    return a1 * a2, a2 * b1 + b2


def rglru(x, wa, ba, wx, bx, lam, h0):
    B, S, _ = x.shape
    x = x.astype(F32)
    xb = x.reshape(B, S, D_BLOCKS, D_BW)
    r = jax.nn.sigmoid(jnp.einsum('bsnd,nde->bsne', xb, wa).reshape(B, S, D_RNN) + ba)
    ig = jax.nn.sigmoid(jnp.einsum('bsnd,nde->bsne', xb, wx).reshape(B, S, D_RNN) + bx)
    log_a = -RG_C * r * jax.nn.softplus(-lam.astype(F32))
    a = jnp.exp(log_a)
    u = jnp.sqrt(-jnp.expm1(2.0 * log_a)) * (ig * x)
    u = u.at[:, 0].add(a[:, 0] * h0.astype(F32))
    _, hs = lax.associative_scan(combine_linear, (a, u), axis=1)
    return hs, hs[:, -1]


def zero_states(B):
    return (jnp.zeros((B, 2, C_HEADS, C_DK, C_DV), F32), jnp.zeros((B, 2, C_HEADS, C_DK), F32),
            jnp.zeros((B, 2, C_HEADS), F32), jnp.zeros((B, 2, D_RNN), F32))


def odd_mixer(h, st_C, st_n, st_m, st_h, w_in, gate_b, conv_c, conv_d, conv_d_b,
              rg_wa, rg_ba, rg_wx, rg_bx, rg_lam, norm_g, w_out):
    B, S, _ = h.shape
    qk, v, o, gates, xd, gd = jnp.split(
        h @ w_in, [2 * C_W, 3 * C_W, 4 * C_W, 4 * C_W + 4 * C_HEADS, 4 * C_W + 4 * C_HEADS + D_RNN], axis=-1)
    q, k = jnp.split(jax.nn.silu(conv_centred(qk, conv_c)), 2, axis=-1)

    def heads(a):
        return a.reshape(B, S, C_HEADS, -1).transpose(0, 2, 1, 3)

    q, k, v = heads(q), heads(k), heads(v)
    g = (gates + gate_b).reshape(B, S, 4, C_HEADS).transpose(2, 0, 3, 1)
    hf, (Cf, nf, mf) = mlstm_chunked(q, k, v, g[0], g[1], st_C[:, 0], st_n[:, 0], st_m[:, 0])
    hb, (Cb, nb, mb) = mlstm_chunked(jnp.flip(q, 2), jnp.flip(k, 2), jnp.flip(v, 2), jnp.flip(g[2], 2),
                                     jnp.flip(g[3], 2), st_C[:, 1], st_n[:, 1], st_m[:, 1])
    hc = rmsnorm(hf + jnp.flip(hb, 2), norm_g)
    hc = hc.transpose(0, 2, 1, 3).reshape(B, S, C_W) * jax.nn.sigmoid(o)
    xc = conv_centred(xd, conv_d) + conv_d_b
    rf, hfin_f = rglru(xc, rg_wa[0], rg_ba[0], rg_wx[0], rg_bx[0], rg_lam[0], st_h[:, 0])
    rb, hfin_b = rglru(jnp.flip(xc, 1), rg_wa[1], rg_ba[1], rg_wx[1], rg_bx[1], rg_lam[1], st_h[:, 1])
    hd = (rf + jnp.flip(rb, 1)) * jax.nn.gelu(gd)
    out = (jnp.concatenate([hc, hd], axis=-1) @ w_out).astype(h.dtype)
    states = (jnp.stack([Cf, Cb], axis=1), jnp.stack([nf, nb], axis=1),
              jnp.stack([mf, mb], axis=1), jnp.stack([hfin_f, hfin_b], axis=1))
    return out, states


def swiglu(x, w1, w3, w2):
    return (jax.nn.silu(x @ w1) * (x @ w3)) @ w2


def moe_swiglu(h, router_w, router_b, w1, w3, w2):
    B, S, D = h.shape
    t = h.reshape(B * S, D)
    logits = (t @ router_w + router_b).astype(F32)
    top_v, top_i = lax.top_k(logits, TOP_K)
    gate = jax.nn.softmax(top_v, axis=-1)
    dense_gate = jnp.sum(jax.nn.one_hot(top_i, N_EXPERTS, dtype=F32) * gate[..., None], axis=1).astype(t.dtype)
    out = jnp.zeros_like(t)
    for e in range(N_EXPERTS):
        out = out + dense_gate[:, e:e + 1] * swiglu(t, w1[e], w3[e], w2[e])
    return out.reshape(B, S, D)


def setup_inputs(seed: int = 0) -> dict:
    key = jax.random.key(seed)
    ks = iter(jax.random.split(key, 64))

    def nrm(shape, scale=1.0):
        return jax.random.normal(next(ks), shape, F32) * scale

    def gain(shape):
        return 1.0 + nrm(shape, 0.05)

    u = jax.random.uniform(next(ks), (N_ODD, 2, D_RNN), F32, 0.9, 0.999)
    s = u ** (1.0 / RG_C)
    rg_lam = jnp.log(s) - jnp.log1p(-s)
    gate_offset = jnp.array([0.0, 4.0, 0.0, 4.0], F32)[:, None]
    c_gate_b = (nrm((N_ODD, 4, C_HEADS), 0.1) + gate_offset).reshape(N_ODD, 4 * C_HEADS)
    return {
        'x_prompt': nrm((BATCH, SEQ, D_MODEL)),
        'x_sample': nrm((DEC_BATCH, DEC_SEQ, D_MODEL)),
        'c': nrm((DEC_BATCH, D_MODEL)),
        'cache_a_k': nrm((DEC_BATCH, N_EVEN, PAST_LEN, A_HEADS, 2 * A_HD)),
        'cache_a_v': nrm((DEC_BATCH, N_EVEN, PAST_LEN, A_HEADS, 2 * A_HD)),
        'cache_b_k': nrm((DEC_BATCH, N_EVEN, PAST_LEN, B_HEADS, B_HD)),
        'cache_b_v': nrm((DEC_BATCH, N_EVEN, PAST_LEN, B_HEADS, B_HD)),
        'state_c_C': nrm((DEC_BATCH, N_ODD, 2, C_HEADS, C_DK, C_DV), 0.5),
        'state_c_n': nrm((DEC_BATCH, N_ODD, 2, C_HEADS, C_DK), 0.5),
        'state_c_m': nrm((DEC_BATCH, N_ODD, 2, C_HEADS)),
        'state_d_h': nrm((DEC_BATCH, N_ODD, 2, D_RNN), 0.5),
        'c_ctx': nrm((D_MODEL,)),
        'mod_w': nrm((DEPTH, D_MODEL, 6 * D_MODEL), 0.5 * D_MODEL ** -0.5),
        'mod_b': nrm((DEPTH, 6 * D_MODEL), 0.02),
        'norm_g': gain((DEPTH, 4, D_MODEL)),
        'w_in_even': nrm((N_EVEN, D_MODEL, IN_EVEN), D_MODEL ** -0.5),
        'w_out_even': nrm((N_EVEN, MIX_EVEN, D_MODEL), MIX_EVEN ** -0.5),
        'a_lam': nrm((N_EVEN, 4, A_HD), 0.1),
        'a_subln_g': gain((N_EVEN, 2 * A_HD)),
        'b_rpb': nrm((N_EVEN, B_HEADS, 2 * NA_ROWS - 1, 2 * NA_COLS - 1), 0.5),
        'ff_w1': nrm((N_EVEN, D_MODEL, FF_DENSE), D_MODEL ** -0.5),
        'ff_w3': nrm((N_EVEN, D_MODEL, FF_DENSE), D_MODEL ** -0.5),
        'ff_w2': nrm((N_EVEN, FF_DENSE, D_MODEL), FF_DENSE ** -0.5),
        'w_in_odd': nrm((N_ODD, D_MODEL, IN_ODD), D_MODEL ** -0.5),
        'c_gate_b': c_gate_b,
        'conv_c': nrm((N_ODD, CONV_W, 2 * C_W), 0.5),
        'conv_d': nrm((N_ODD, CONV_W, D_RNN), 0.5),
        'conv_d_b': nrm((N_ODD, D_RNN), 0.02),
        'rg_wa': nrm((N_ODD, 2, D_BLOCKS, D_BW, D_BW), D_BW ** -0.5),
        'rg_ba': nrm((N_ODD, 2, D_RNN), 0.1),
        'rg_wx': nrm((N_ODD, 2, D_BLOCKS, D_BW, D_BW), D_BW ** -0.5),
        'rg_bx': nrm((N_ODD, 2, D_RNN), 0.1),
        'rg_lam': rg_lam,
        'c_norm_g': gain((N_ODD, C_DV)),
        'w_out_odd': nrm((N_ODD, MIX_ODD, D_MODEL), MIX_ODD ** -0.5),
        'router_w': nrm((N_ODD, D_MODEL, N_EXPERTS), D_MODEL ** -0.5),
        'router_b': nrm((N_ODD, N_EXPERTS), 0.01),
        'moe_w1': nrm((N_ODD, N_EXPERTS, D_MODEL, FF_EXPERT), D_MODEL ** -0.5),
        'moe_w3': nrm((N_ODD, N_EXPERTS, D_MODEL, FF_EXPERT), D_MODEL ** -0.5),
        'moe_w2': nrm((N_ODD, N_EXPERTS, FF_EXPERT, D_MODEL), FF_EXPERT ** -0.5),
    }


def reference(x_prompt, x_sample, c, cache_a_k, cache_a_v, cache_b_k, cache_b_v,
              state_c_C, state_c_n, state_c_m, state_d_h, c_ctx, mod_w, mod_b, norm_g,
              w_in_even, w_out_even, a_lam, a_subln_g, b_rpb, ff_w1, ff_w3, ff_w2,
              w_in_odd, c_gate_b, conv_c, conv_d, conv_d_b, rg_wa, rg_ba, rg_wx, rg_bx, rg_lam,
              c_norm_g, w_out_odd, router_w, router_b, moe_w1, moe_w3, moe_w2):
    y_p = x_prompt
    y_s = x_sample
    new_a_k, new_a_v, new_b_k, new_b_v = [], [], [], []
    new_c_C, new_c_n, new_c_m, new_d_h = [], [], [], []
    for l in range(DEPTH):
        mp = adaln(c_ctx[None], mod_w[l], mod_b[l])
        ms = adaln(c, mod_w[l], mod_b[l])
        g = norm_g[l]
        if l % 2 == 0:
            e = l // 2
            lam_init = 0.8 - 0.6 * math.exp(-0.3 * l)
            ew = (w_in_even[e], w_out_even[e], a_lam[e], a_subln_g[e], lam_init)
            y_p, kv = sublayer(y_p, mp[0], mp[1], mp[2], g[0], g[1],
                               lambda h: even_mixer_context(h, *ew))
            y_s, _ = sublayer(y_s, ms[0], ms[1], ms[2], g[0], g[1],
                              lambda h: even_mixer_latent(h, cache_a_k[:, e], cache_a_v[:, e], cache_b_k[:, e],
                                                          cache_b_v[:, e], b_rpb[e], *ew))
            new_a_k.append(kv[0])
            new_a_v.append(kv[1])
            new_b_k.append(kv[2])
            new_b_v.append(kv[3])
            ffn = lambda h: (swiglu(h, ff_w1[e], ff_w3[e], ff_w2[e]), None)
        else:
            j = l // 2
            ow = (w_in_odd[j], c_gate_b[j], conv_c[j], conv_d[j], conv_d_b[j], rg_wa[j], rg_ba[j],
                  rg_wx[j], rg_bx[j], rg_lam[j], c_norm_g[j], w_out_odd[j])
            y_p, st = sublayer(y_p, mp[0], mp[1], mp[2], g[0], g[1],
                               lambda h: odd_mixer(h, *zero_states(h.shape[0]), *ow))
            y_s, _ = sublayer(y_s, ms[0], ms[1], ms[2], g[0], g[1],
                              lambda h: odd_mixer(h, state_c_C[:, j], state_c_n[:, j], state_c_m[:, j],
                                                  state_d_h[:, j], *ow))
            new_c_C.append(st[0])
            new_c_n.append(st[1])
            new_c_m.append(st[2])
            new_d_h.append(st[3])
            ffn = lambda h: (moe_swiglu(h, router_w[j], router_b[j], moe_w1[j], moe_w3[j], moe_w2[j]), None)
        y_p, _ = sublayer(y_p, mp[3], mp[4], mp[5], g[2], g[3], ffn)
        y_s, _ = sublayer(y_s, ms[3], ms[4], ms[5], g[2], g[3], ffn)
    new_a_k = jnp.stack(new_a_k, axis=1)
    new_a_v = jnp.stack(new_a_v, axis=1)
    new_b_k = jnp.stack(new_b_k, axis=1)
    new_b_v = jnp.stack(new_b_v, axis=1)
    new_c_C = jnp.stack(new_c_C, axis=1)
    new_c_n = jnp.stack(new_c_n, axis=1)
    new_c_m = jnp.stack(new_c_m, axis=1)
    new_d_h = jnp.stack(new_d_h, axis=1)
    return (y_p, y_s, new_a_k, new_a_v, new_b_k, new_b_v, new_c_C, new_c_n, new_c_m, new_d_h)
```

```python
import functools
import math

import numpy as np
import jax
import jax.numpy as jnp
from jax import lax
from jax.experimental import pallas as pl
from jax.experimental.pallas import tpu as pltpu

D_MODEL = 1024
BATCH = 32
SEQ = 256
DEPTH = 4
DEC_BATCH = 4
DEC_SEQ = 1024
PAST_LEN = 256
GRID_W = 64
GRID_ROWS = DEC_SEQ // GRID_W
N_EVEN = (DEPTH + 1) // 2
N_ODD = DEPTH // 2
A_HEADS = 4
A_HD = 64
B_HEADS = 8
B_HD = 64
NA_ROWS = 8
NA_COLS = 16
C_HEADS = 4
C_DK = 128
C_DV = 128
C_CHUNK = 64
D_RNN = 512
D_BLOCKS = 8
D_BW = D_RNN // D_BLOCKS
RG_C = 8.0
CONV_W = 4
FF_DENSE = 2816
N_EXPERTS = 8
FF_EXPERT = 2048
ROPE_BASE = 10000.0
EPS = 1e-6
NEG_INF = -1e30
A_W = A_HEADS * 2 * A_HD
B_W = B_HEADS * B_HD
C_W = C_HEADS * C_DK
IN_EVEN = 3 * A_W + 3 * B_W
IN_ODD_PAD = 4 * C_W + 2 * D_RNN + 128
F32 = jnp.float32
BF16 = jnp.bfloat16

N_CTX = BATCH * SEQ
N_LAT = DEC_BATCH * DEC_SEQ
N_TOK = N_CTX + N_LAT
MOD_ROWS = 8
LANES = 128
VMEM_LIMIT = 56 * 1024 * 1024
HI = lax.Precision.HIGHEST


def _cparams(n_axes):
    return pltpu.CompilerParams(dimension_semantics=("arbitrary",) * n_axes,
                                vmem_limit_bytes=VMEM_LIMIT)


def _sigmoid(x):
    return 1.0 / (1.0 + jnp.exp(-x))


def _silu(x):
    return x * _sigmoid(x)


def _softplus(x):
    return jnp.maximum(x, 0.0) + jnp.log1p(jnp.exp(-jnp.abs(x)))


def _gelu_tanh(x):
    return 0.5 * x * (1.0 + jnp.tanh(math.sqrt(2.0 / math.pi) * (x + 0.044715 * (x * x * x))))


def _rms(x, g):
    return x * lax.rsqrt(jnp.mean(x * x, axis=-1, keepdims=True) + EPS) * g


def _norm_mod(x, g, scale, shift):
    return _rms(x, g) * (1.0 + scale) + shift


def _dot(a, b):
    return jnp.dot(a, b, preferred_element_type=F32)


def _dot_nt(a, b):
    return lax.dot_general(a, b, (((1,), (1,)), ((), ())), preferred_element_type=F32)


def _dot_tn(a, b):
    return lax.dot_general(a, b, (((0,), (0,)), ((), ())), preferred_element_type=F32)


def _softmax_rows(s):
    e = jnp.exp(s - jnp.max(s, axis=-1, keepdims=True))
    return e / jnp.sum(e, axis=-1, keepdims=True)


def _mod_row(i, tm):
    r0 = i * tm
    return jnp.where(r0 < N_CTX, 0, 1 + (r0 - N_CTX) // DEC_SEQ)


def _mod_spec(which, tm):
    return pl.BlockSpec((1, 1, D_MODEL), lambda i, *_: (which * MOD_ROWS + _mod_row(i, tm), 0, 0))


def _adaln_kernel(c_ref, w_ref, b_ref, o_ref):
    s = _silu(c_ref[...]).astype(BF16)
    o_ref[0] = _dot(s, w_ref[0].astype(BF16)) + b_ref[0]


def adaln_all(cvec, mod_w, mod_b):
    tn = 1536
    out = pl.pallas_call(
        _adaln_kernel,
        grid=(DEPTH, 6 * D_MODEL // tn),
        in_specs=[pl.BlockSpec((MOD_ROWS, D_MODEL), lambda l, j: (0, 0)),
                  pl.BlockSpec((1, D_MODEL, tn), lambda l, j: (l, 0, j)),
                  pl.BlockSpec((1, 1, tn), lambda l, j: (l, 0, j))],
        out_specs=pl.BlockSpec((1, MOD_ROWS, tn), lambda l, j: (l, 0, j)),
        out_shape=jax.ShapeDtypeStruct((DEPTH, MOD_ROWS, 6 * D_MODEL), F32),
        compiler_params=_cparams(2),
        name="adaln",
    )(cvec, mod_w, mod_b.reshape(DEPTH, 1, 6 * D_MODEL))
    out = out.reshape(DEPTH, MOD_ROWS, 6, D_MODEL).transpose(0, 2, 1, 3)
    return out.reshape(DEPTH, 6 * MOD_ROWS, 1, D_MODEL)


def _nm_matmul_kernel(x_ref, sh_ref, sc_ref, g_ref, w_ref, o_ref, h_ref):
    @pl.when(pl.program_id(1) == 0)
    def _():
        h_ref[...] = _norm_mod(x_ref[...], g_ref[...], sc_ref[0], sh_ref[0]).astype(BF16)

    o_ref[...] = _dot(h_ref[...], w_ref[...])


def nm_matmul(x, mod, g_pre, w, *, tm, tn):
    n = w.shape[1]
    return pl.pallas_call(
        _nm_matmul_kernel,
        grid=(N_TOK // tm, n // tn),
        in_specs=[pl.BlockSpec((tm, D_MODEL), lambda i, j: (i, 0)),
                  _mod_spec(0, tm), _mod_spec(1, tm),
                  pl.BlockSpec((1, D_MODEL), lambda i, j: (0, 0)),
                  pl.BlockSpec((D_MODEL, tn), lambda i, j: (0, j))],
        out_specs=pl.BlockSpec((tm, tn), lambda i, j: (i, j)),
        out_shape=jax.ShapeDtypeStruct((N_TOK, n), F32),
        scratch_shapes=[pltpu.VMEM((tm, D_MODEL), BF16)],
        compiler_params=_cparams(2),
        name="nm_matmul",
    )(x, mod, mod, g_pre.reshape(1, D_MODEL), w)


def _out_res_kernel(a_ref, w_ref, x_ref, gate_ref, g_ref, o_ref):
    out = _dot(a_ref[...].astype(BF16), w_ref[...])
    o_ref[...] = x_ref[...] + gate_ref[0] * _rms(out, g_ref[...])


def out_residual(a, w, x, mod, g_post, *, tm):
    k = a.shape[1]
    return pl.pallas_call(
        _out_res_kernel,
        grid=(N_TOK // tm,),
        in_specs=[pl.BlockSpec((tm, k), lambda i: (i, 0)),
                  pl.BlockSpec((k, D_MODEL), lambda i: (0, 0)),
                  pl.BlockSpec((tm, D_MODEL), lambda i: (i, 0)),
                  _mod_spec(2, tm),
                  pl.BlockSpec((1, D_MODEL), lambda i: (0, 0))],
        out_specs=pl.BlockSpec((tm, D_MODEL), lambda i: (i, 0)),
        out_shape=jax.ShapeDtypeStruct((N_TOK, D_MODEL), F32),
        compiler_params=_cparams(1),
        name="out_residual",
    )(a, w, x, mod, g_post.reshape(1, D_MODEL))


def _lam_value(lam_ref, lam_init):
    lv = lam_ref[...]
    a = jnp.exp(jnp.sum(lv[0:1] * lv[1:2], axis=-1, keepdims=True))
    b = jnp.exp(jnp.sum(lv[2:3] * lv[3:4], axis=-1, keepdims=True))
    return a - b + lam_init


def _diff_attention(q, k, v, lam, subln_g, lam_init):
    outs = []
    for h in range(A_HEADS):
        ps = []
        for m in range(2):
            lo = (2 * h + m) * A_HD
            s = _dot_nt(q[:, lo:lo + A_HD], k[:, lo:lo + A_HD]) * (A_HD ** -0.5)
            ps.append(_softmax_rows(s))
        p = ps[0] - lam * ps[1]
        o = _dot(p.astype(BF16), v[:, 2 * A_HD * h:2 * A_HD * (h + 1)])
        outs.append(_rms(o, subln_g) * (1.0 - lam_init))
    return outs


def _attn_ctx_kernel(qa_ref, ka_ref, va_ref, qb_ref, kb_ref, vb_ref, lam_ref, sg_ref, o_ref, *, lam_init):
    lam = _lam_value(lam_ref, lam_init)
    outs = _diff_attention(qa_ref[...].astype(BF16), ka_ref[...].astype(BF16), va_ref[...].astype(BF16),
                           lam, sg_ref[...], lam_init)
    for h in range(A_HEADS):
        o_ref[:, 2 * A_HD * h:2 * A_HD * (h + 1)] = outs[h]
    qb = qb_ref[...].astype(BF16)
    kb = kb_ref[...].astype(BF16)
    vb = vb_ref[...].astype(BF16)
    for h in range(B_HEADS):
        sl = slice(B_HD * h, B_HD * (h + 1))
        p = _softmax_rows(_dot_nt(qb[:, sl], kb[:, sl]) * (B_HD ** -0.5))
        o_ref[:, A_W + B_HD * h:A_W + B_HD * (h + 1)] = _dot(p.astype(BF16), vb[:, sl])


def attn_ctx(proj, a_lam, subln_g, lam_init):
    specs = [pl.BlockSpec((SEQ, A_W), lambda b, c=c: (b, c)) for c in range(6)]
    return pl.pallas_call(
        functools.partial(_attn_ctx_kernel, lam_init=lam_init),
        grid=(BATCH,),
        in_specs=specs + [pl.BlockSpec((4, A_HD), lambda b: (0, 0)),
                          pl.BlockSpec((1, 2 * A_HD), lambda b: (0, 0))],
        out_specs=pl.BlockSpec((SEQ, A_W + B_W), lambda b: (b, 0)),
        out_shape=jax.ShapeDtypeStruct((N_CTX, A_W + B_W), F32),
        compiler_params=_cparams(1),
        name="attn_ctx",
    )(proj, proj, proj, proj, proj, proj, a_lam, subln_g.reshape(1, 2 * A_HD))


def _rope_tables():
    half = A_HD // 2
    t = np.arange(DEC_SEQ)
    pos = np.stack([t // GRID_W, t % GRID_W], axis=-1).astype(np.float32)
    freqs = (np.float32(ROPE_BASE) ** (-np.arange(0, half, 2, dtype=np.float32) / np.float32(half))).astype(np.float32)
    ang = (pos[:, :, None] * freqs).astype(np.float32)
    cos = np.cos(ang).astype(np.float32)
    sin = np.sin(ang).astype(np.float32)
    cos_h = np.concatenate([cos[:, 0], cos[:, 0], cos[:, 1], cos[:, 1]], axis=-1)
    sin_h = np.concatenate([-sin[:, 0], sin[:, 0], -sin[:, 1], sin[:, 1]], axis=-1)
    reps = A_W // A_HD
    return np.tile(cos_h, (1, reps)), np.tile(sin_h, (1, reps))


def _rope(x, cos, sin):
    quarter = A_HD // 4
    cols = []
    for c in range(x.shape[1] // LANES):
        xs = x[:, c * LANES:(c + 1) * LANES]
        up = pltpu.roll(xs, LANES - quarter, 1)
        dn = pltpu.roll(xs, quarter, 1)
        lane = lax.broadcasted_iota(jnp.int32, xs.shape, 1)
        cols.append(jnp.where((lane % (2 * quarter)) < quarter, up, dn))
    partner = jnp.concatenate(cols, axis=1)
    return x * cos + partner * sin


def _attn_lat_a_kernel(q_ref, k_ref, v_ref, ck_ref, cv_ref, cosq_ref, sinq_ref, cosk_ref, sink_ref,
                       lam_ref, sg_ref, o_ref, k_s, v_s, *, lam_init):
    @pl.when(pl.program_id(1) == 0)
    def _():
        k_s[0:DEC_SEQ, :] = _rope(k_ref[...], cosk_ref[...], sink_ref[...]).astype(BF16)
        k_s[DEC_SEQ:DEC_SEQ + PAST_LEN, :] = ck_ref[...].astype(BF16)
        v_s[0:DEC_SEQ, :] = v_ref[...].astype(BF16)
        v_s[DEC_SEQ:DEC_SEQ + PAST_LEN, :] = cv_ref[...].astype(BF16)

    lam = _lam_value(lam_ref, lam_init)
    q = _rope(q_ref[...], cosq_ref[...], sinq_ref[...]).astype(BF16)
    outs = _diff_attention(q, k_s[...], v_s[...], lam, sg_ref[...], lam_init)
    for h in range(A_HEADS):
        o_ref[:, 2 * A_HD * h:2 * A_HD * (h + 1)] = outs[h]


def attn_lat_a(proj, cache_k, cache_v, e, a_lam, subln_g, lam_init, *, tq):
    cos, sin = _rope_tables()
    nq = DEC_SEQ // tq
    q0 = N_CTX // tq
    b0 = N_CTX // DEC_SEQ
    cache_spec = pl.BlockSpec((None, None, PAST_LEN, A_W), lambda b, i: (b, e, 0, 0))
    return pl.pallas_call(
        functools.partial(_attn_lat_a_kernel, lam_init=lam_init),
        grid=(DEC_BATCH, nq),
        in_specs=[pl.BlockSpec((tq, A_W), lambda b, i: (q0 + b * nq + i, 0)),
                  pl.BlockSpec((DEC_SEQ, A_W), lambda b, i: (b0 + b, 1)),
                  pl.BlockSpec((DEC_SEQ, A_W), lambda b, i: (b0 + b, 2)),
                  cache_spec, cache_spec,
                  pl.BlockSpec((tq, A_W), lambda b, i: (i, 0)),
                  pl.BlockSpec((tq, A_W), lambda b, i: (i, 0)),
                  pl.BlockSpec((DEC_SEQ, A_W), lambda b, i: (0, 0)),
                  pl.BlockSpec((DEC_SEQ, A_W), lambda b, i: (0, 0)),
                  pl.BlockSpec((4, A_HD), lambda b, i: (0, 0)),
                  pl.BlockSpec((1, 2 * A_HD), lambda b, i: (0, 0))],
        out_specs=pl.BlockSpec((tq, A_W), lambda b, i: (b * nq + i, 0)),
        out_shape=jax.ShapeDtypeStruct((N_LAT, A_W), F32),
        scratch_shapes=[pltpu.VMEM((DEC_SEQ + PAST_LEN, A_W), BF16),
                        pltpu.VMEM((DEC_SEQ + PAST_LEN, A_W), BF16)],
        compiler_params=_cparams(2),
        name="attn_lat_a",
    )(proj, proj, proj, cache_k, cache_v, cos, sin, cos, sin, a_lam, subln_g.reshape(1, 2 * A_HD))


NA_WIN = NA_ROWS * GRID_W


def _na_bias(rpb):
    r = np.arange(GRID_ROWS)
    r0 = np.clip(r - NA_ROWS // 2, 0, GRID_ROWS - NA_ROWS)
    krow = r0[:, None] + np.arange(NA_ROWS)
    dr = krow - r[:, None] + NA_ROWS - 1
    qcol = np.arange(GRID_W)
    kcol = np.arange(GRID_W)
    cstart = np.clip(qcol - NA_COLS // 2, 0, GRID_W - NA_COLS)
    valid = (kcol[None, :] >= cstart[:, None]) & (kcol[None, :] < cstart[:, None] + NA_COLS)
    dc = np.clip(kcol[None, :] - qcol[:, None] + NA_COLS - 1, 0, 2 * NA_COLS - 2)
    bias = rpb.astype(F32)[:, dr[:, None, :, None], dc[None, :, None, :]]
    bias = jnp.where(valid[None, None, :, None, :], bias, NEG_INF)
    return bias.reshape(B_HEADS, GRID_ROWS, GRID_W, NA_WIN)


def _attn_lat_b_kernel(q_ref, k_ref, v_ref, ck_ref, cv_ref, bias_ref, o_ref):
    r = pl.program_id(1)
    r0 = jnp.clip(r - NA_ROWS // 2, 0, GRID_ROWS - NA_ROWS)
    win = pl.ds(pl.multiple_of(r0 * GRID_W, GRID_W), NA_WIN)
    q = q_ref[...].astype(BF16)
    kw = k_ref[win, :].astype(BF16)
    vw = v_ref[win, :].astype(BF16)
    kc = ck_ref[...].astype(BF16)
    vc = cv_ref[...].astype(BF16)
    scale = B_HD ** -0.5
    for h in range(B_HEADS):
        sl = slice(B_HD * h, B_HD * (h + 1))
        s_win = _dot_nt(q[:, sl], kw[:, sl]) * scale + bias_ref[h, 0]
        s_ctx = _dot_nt(q[:, sl], kc[:, sl]) * scale
        m = jnp.maximum(jnp.max(s_win, axis=-1, keepdims=True), jnp.max(s_ctx, axis=-1, keepdims=True))
        e_win = jnp.exp(s_win - m)
        e_ctx = jnp.exp(s_ctx - m)
        den = jnp.sum(e_win, axis=-1, keepdims=True) + jnp.sum(e_ctx, axis=-1, keepdims=True)
        p_win = (e_win / den).astype(BF16)
        p_ctx = (e_ctx / den).astype(BF16)
        o_ref[:, sl] = _dot(p_win, vw[:, sl]) + _dot(p_ctx, vc[:, sl])


def attn_lat_b(proj, cache_k, cache_v, e, rpb):
    bias = _na_bias(rpb)
    q0 = N_CTX // GRID_W
    b0 = N_CTX // DEC_SEQ
    cache_spec = pl.BlockSpec((None, None, PAST_LEN, B_W), lambda b, r: (b, e, 0, 0))
    return pl.pallas_call(
        _attn_lat_b_kernel,
        grid=(DEC_BATCH, GRID_ROWS),
        in_specs=[pl.BlockSpec((GRID_W, B_W), lambda b, r: (q0 + b * GRID_ROWS + r, 3)),
                  pl.BlockSpec((DEC_SEQ, B_W), lambda b, r: (b0 + b, 4)),
                  pl.BlockSpec((DEC_SEQ, B_W), lambda b, r: (b0 + b, 5)),
                  cache_spec, cache_spec,
                  pl.BlockSpec((B_HEADS, 1, GRID_W, NA_WIN), lambda b, r: (0, r, 0, 0))],
        out_specs=pl.BlockSpec((GRID_W, B_W), lambda b, r: (b * GRID_ROWS + r, 0)),
        out_shape=jax.ShapeDtypeStruct((N_LAT, B_W), F32),
        compiler_params=_cparams(2),
        name="attn_lat_b",
    )(proj, proj, proj, cache_k, cache_v, bias)


CONV_PAD = 8


def _conv_centred(x_ref, pad_ref, w_ref, seq):
    c = pad_ref.shape[1]
    pad_ref[0:CONV_PAD, :] = jnp.zeros((CONV_PAD, c), F32)
    pad_ref[CONV_PAD + seq:2 * CONV_PAD + seq, :] = jnp.zeros((CONV_PAD, c), F32)
    pad_ref[CONV_PAD:CONV_PAD + seq, :] = x_ref[...]
    left = (CONV_W - 1) // 2
    acc = None
    for j in range(CONV_W):
        start = CONV_PAD - left + j
        term = w_ref[j:j + 1, :] * pad_ref[start:start + seq, :]
        acc = term if acc is None else acc + term
    return acc


def _mlstm_kernel(qk_ref, v_ref, o_ref, g_ref, gt_ref, gb_ref, gbt_ref, cw_ref, ng_ref,
                  c0_ref, n0_ref, m0_ref, out_ref, cf_ref, nf_ref, mf_ref,
                  pad_s, qk_s, acc_s, c_s, n_s, m_s, *, seq):
    L = C_CHUNK
    nc = seq // L
    qk_s[...] = _silu(_conv_centred(qk_ref, pad_s, cw_ref, seq))
    row = lax.broadcasted_iota(jnp.int32, (L, L), 0)
    col = lax.broadcasted_iota(jnp.int32, (L, L), 1)
    gb = gb_ref[...]
    gbt = gbt_ref[...]
    ng = ng_ref[...]
    for d in range(2):
        tri = (col <= row) if d == 0 else (col >= row)
        tri_f = tri.astype(F32)
        tri_t = ((row <= col) if d == 0 else (row >= col)).astype(F32)
        last = L - 1 if d == 0 else 0
        for h in range(C_HEADS):
            c_s[h] = c0_ref[d * C_HEADS + h]
        n_s[...] = n0_ref[d * C_HEADS:(d + 1) * C_HEADS]
        m_s[...] = m0_ref[d * C_HEADS:(d + 1) * C_HEADS]

        def chunk_step(ci, carry, d=d, tri=tri, tri_f=tri_f, tri_t=tri_t, last=last):
            c_idx = ci if d == 0 else nc - 1 - ci
            rows = pl.ds(pl.multiple_of(c_idx * L, L), L)
            g = g_ref[rows, :] + gb
            gt = gt_ref[c_idx] + gbt
            ls = -_softplus(-g)
            lst = -_softplus(-gt)
            bcol = jnp.dot(tri_f, ls, precision=HI, preferred_element_type=F32)
            brow = jnp.dot(lst, tri_t, precision=HI, preferred_element_type=F32)
            for h in range(C_HEADS):
                ci_ = d * 2 * C_HEADS + h
                cf_ = ci_ + C_HEADS
                b_c = bcol[:, cf_:cf_ + 1]
                b_r = brow[cf_:cf_ + 1, :]
                i_c = g[:, ci_:ci_ + 1]
                i_r = gt[ci_:ci_ + 1, :]
                m_prev = m_s[h][:, 0:1]
                n_prev = n_s[h]
                c_prev = c_s[h]
                dlog = jnp.where(tri, b_c - b_r + i_r, -jnp.inf)
                inter = b_c + m_prev
                m_t = jnp.maximum(inter, jnp.max(dlog, axis=-1, keepdims=True))
                w_in = jnp.exp(dlog - m_t)
                w_st = jnp.exp(inter - m_t)
                q = qk_s[rows, C_DK * h:C_DK * (h + 1)] * (C_DK ** -0.5)
                k = qk_s[rows, C_W + C_DK * h:C_W + C_DK * (h + 1)]
                v = v_ref[rows, C_DV * h:C_DV * (h + 1)]
                qb = q.astype(BF16)
                kb = k.astype(BF16)
                vb = v.astype(BF16)
                s = _dot_nt(qb, kb) * w_in
                num = w_st * _dot(qb, c_prev.astype(BF16)) + _dot(s.astype(BF16), vb)
                nq = w_st * jnp.sum(q * n_prev, axis=-1, keepdims=True) + jnp.sum(s, axis=-1, keepdims=True)
                hh = num / jnp.maximum(jnp.abs(nq), jnp.exp(-m_t))
                b_last = b_c[last:last + 1, :]
                wlog = b_last - b_c + i_c
                m_new = jnp.maximum(b_last + m_prev, jnp.max(wlog, axis=0, keepdims=True))
                ws = jnp.exp(wlog - m_new)
                wc = jnp.exp(b_last + m_prev - m_new)
                kw = ws * k
                c_s[h] = wc * c_prev + _dot_tn(kw.astype(BF16), vb)
                n_s[h] = wc * n_prev + jnp.sum(kw, axis=0, keepdims=True)
                m_s[h] = jnp.broadcast_to(m_new, (1, LANES))
                hs = slice(C_DV * h, C_DV * (h + 1))
                if d == 0:
                    acc_s[rows, hs] = hh
                else:
                    tot = acc_s[rows, hs] + hh
                    out_ref[rows, hs] = _rms(tot, ng) * _sigmoid(o_ref[rows, hs])
            return carry

        lax.fori_loop(0, nc, chunk_step, 0)
        for h in range(C_HEADS):
            cf_ref[d * C_HEADS + h] = c_s[h]
        nf_ref[d * C_HEADS:(d + 1) * C_HEADS] = n_s[...]
        mf_ref[d * C_HEADS:(d + 1) * C_HEADS] = m_s[...]


def mlstm(proj, gates_t, gate_b, conv_c, norm_g, c0, n0, m0, *, seq, nb, row0):
    nh = 2 * C_HEADS
    b0 = row0 // seq
    nc = seq // C_CHUNK
    gate_cols = (IN_ODD_PAD - LANES) // LANES
    gb = jnp.zeros((1, LANES), F32).at[0, :4 * C_HEADS].set(gate_b)
    out_shapes = (jax.ShapeDtypeStruct((nb * seq, C_W), F32),
                  jax.ShapeDtypeStruct((nb, nh, C_DK, C_DV), F32),
                  jax.ShapeDtypeStruct((nb, nh, 1, C_DK), F32),
                  jax.ShapeDtypeStruct((nb, nh, 1, LANES), F32))
    state_specs = [pl.BlockSpec((None, nh, C_DK, C_DV), lambda b: (b, 0, 0, 0)),
                   pl.BlockSpec((None, nh, 1, C_DK), lambda b: (b, 0, 0, 0)),
                   pl.BlockSpec((None, nh, 1, LANES), lambda b: (b, 0, 0, 0))]
    return pl.pallas_call(
        functools.partial(_mlstm_kernel, seq=seq),
        grid=(nb,),
        in_specs=[pl.BlockSpec((seq, 2 * C_W), lambda b: (b0 + b, 0)),
                  pl.BlockSpec((seq, C_W), lambda b: (b0 + b, 2)),
                  pl.BlockSpec((seq, C_W), lambda b: (b0 + b, 3)),
                  pl.BlockSpec((seq, LANES), lambda b: (b0 + b, gate_cols)),
                  pl.BlockSpec((nc, 4 * C_HEADS, C_CHUNK), lambda b: (b, 0, 0)),
                  pl.BlockSpec((1, LANES), lambda b: (0, 0)),
                  pl.BlockSpec((4 * C_HEADS, 1), lambda b: (0, 0)),
                  pl.BlockSpec((CONV_W, 2 * C_W), lambda b: (0, 0)),
                  pl.BlockSpec((1, C_DV), lambda b: (0, 0))] + state_specs,
        out_specs=[pl.BlockSpec((seq, C_W), lambda b: (b, 0))] + state_specs,
        out_shape=out_shapes,
        scratch_shapes=[pltpu.VMEM((seq + 2 * CONV_PAD, 2 * C_W), F32),
                        pltpu.VMEM((seq, 2 * C_W), F32),
                        pltpu.VMEM((seq, C_W), F32),
                        pltpu.VMEM((C_HEADS, C_DK, C_DV), F32),
                        pltpu.VMEM((C_HEADS, 1, C_DK), F32),
                        pltpu.VMEM((C_HEADS, 1, LANES), F32)],
        compiler_params=_cparams(1),
        name="mlstm",
    )(proj, proj, proj, proj, gates_t, gb, gate_b.reshape(4 * C_HEADS, 1), conv_c,
      norm_g.reshape(1, C_DV), c0, n0, m0)


RG_BLOCK = 64


def _rglru_kernel(xd_ref, gd_ref, cw_ref, cb_ref, wa_ref, wx_ref, ba_ref, bx_ref, lam_ref, h0_ref,
                  out_ref, hf_ref, pad_s, xc_s, acc_s, *, seq):
    R = RG_BLOCK
    nblk = seq // R
    xc_s[...] = _conv_centred(xd_ref, pad_s, cw_ref, seq) + cb_ref[...]
    row = lax.broadcasted_iota(jnp.int32, (R, D_RNN), 0)
    for d in range(2):
        sp = _softplus(-lam_ref[d])
        wa = wa_ref[d]
        wx = wx_ref[d]
        ba = ba_ref[d]
        bx = bx_ref[d]

        def block_step(bi, carry, d=d, sp=sp, wa=wa, wx=wx, ba=ba, bx=bx):
            b_idx = bi if d == 0 else nblk - 1 - bi
            rows = pl.ds(pl.multiple_of(b_idx * R, R), R)
            x = xc_s[rows, :]
            xb = x.astype(BF16)
            r = _sigmoid(_dot(xb, wa) + ba)
            ig = _sigmoid(_dot(xb, wx) + bx)
            log_a = -RG_C * r * sp
            a = jnp.exp(log_a)
            u = jnp.sqrt(-jnp.tanh(log_a) * (a * a + 1.0)) * (ig * x)
            sh = 1
            while sh < R:
                if d == 0:
                    a_s = pltpu.roll(a, sh, 0)
                    u_s = pltpu.roll(u, sh, 0)
                    ok = row >= sh
                else:
                    a_s = pltpu.roll(a, R - sh, 0)
                    u_s = pltpu.roll(u, R - sh, 0)
                    ok = row < R - sh
                u = u + a * jnp.where(ok, u_s, 0.0)
                a = a * jnp.where(ok, a_s, 1.0)
                sh *= 2
            hs = u + a * carry
            if d == 0:
                acc_s[rows, :] = hs
                return hs[R - 1:R, :]
            out_ref[rows, :] = (acc_s[rows, :] + hs) * _gelu_tanh(gd_ref[rows, :])
            return hs[0:1, :]

        hf_ref[d] = lax.fori_loop(0, nblk, block_step, h0_ref[d])


def _block_diag(w):
    eye = jnp.eye(D_BLOCKS, dtype=w.dtype)
    return (w[:, :, None, :] * eye[:, None, :, None]).reshape(D_RNN, D_RNN)


def rglru(proj, conv_d, conv_d_b, rg_wa, rg_ba, rg_wx, rg_bx, rg_lam, h0, *, seq, nb, row0):
    b0 = row0 // seq
    wa = jnp.stack([_block_diag(rg_wa[0]), _block_diag(rg_wa[1])]).astype(BF16)
    wx = jnp.stack([_block_diag(rg_wx[0]), _block_diag(rg_wx[1])]).astype(BF16)
    vec = lambda a: a.reshape(2, 1, D_RNN)
    full = lambda shape: pl.BlockSpec(shape, lambda b: (0,) * len(shape))
    return pl.pallas_call(
        functools.partial(_rglru_kernel, seq=seq),
        grid=(nb,),
        in_specs=[pl.BlockSpec((seq, D_RNN), lambda b: (b0 + b, 4)),
                  pl.BlockSpec((seq, D_RNN), lambda b: (b0 + b, 5)),
                  full((CONV_W, D_RNN)), full((1, D_RNN)),
                  full((2, D_RNN, D_RNN)), full((2, D_RNN, D_RNN)),
                  full((2, 1, D_RNN)), full((2, 1, D_RNN)), full((2, 1, D_RNN)),
                  pl.BlockSpec((None, 2, 1, D_RNN), lambda b: (b, 0, 0, 0))],
        out_specs=[pl.BlockSpec((seq, D_RNN), lambda b: (b, 0)),
                   pl.BlockSpec((None, 2, 1, D_RNN), lambda b: (b, 0, 0, 0))],
        out_shape=(jax.ShapeDtypeStruct((nb * seq, D_RNN), F32),
                   jax.ShapeDtypeStruct((nb, 2, 1, D_RNN), F32)),
        scratch_shapes=[pltpu.VMEM((seq + 2 * CONV_PAD, D_RNN), F32),
                        pltpu.VMEM((seq, D_RNN), F32),
                        pltpu.VMEM((seq, D_RNN), F32)],
        compiler_params=_cparams(1),
        name="rglru",
    )(proj, proj, conv_d, conv_d_b.reshape(1, D_RNN), wa, wx, vec(rg_ba), vec(rg_bx), vec(rg_lam), h0)


def _router_gates(h, rw_ref, rb_ref):
    logits = jnp.dot(h, rw_ref[...], precision=HI, preferred_element_type=F32) + rb_ref[...]
    lane = lax.broadcasted_iota(jnp.int32, logits.shape, 1)
    logits = jnp.where(lane < N_EXPERTS, logits, -jnp.inf)
    m1 = jnp.max(logits, axis=-1, keepdims=True)
    i1 = jnp.min(jnp.where(logits == m1, lane, LANES), axis=-1, keepdims=True)
    rest = jnp.where(lane == i1, -jnp.inf, logits)
    m2 = jnp.max(rest, axis=-1, keepdims=True)
    i2 = jnp.min(jnp.where(rest == m2, lane, LANES), axis=-1, keepdims=True)
    e2 = jnp.exp(m2 - m1)
    den = 1.0 + e2
    return jnp.where(lane == i1, 1.0 / den, 0.0) + jnp.where(lane == i2, e2 / den, 0.0)


def _swiglu_kernel(x_ref, sh_ref, sc_ref, gate_ref, gpre_ref, gpost_ref, rw_ref, rb_ref,
                   w1_ref, w3_ref, w2_ref, o_ref, h_s, acc_s, dg_s, *, routed):
    e = pl.program_id(1)
    f = pl.program_id(2)

    @pl.when((e == 0) & (f == 0))
    def _():
        h = _norm_mod(x_ref[...], gpre_ref[...], sc_ref[0], sh_ref[0])
        h_s[...] = h.astype(BF16)
        acc_s[...] = jnp.zeros_like(acc_s)
        if routed:
            dg_s[...] = _router_gates(h, rw_ref, rb_ref)

    h = h_s[...]
    u = _silu(_dot(h, w1_ref[0])) * _dot(h, w3_ref[0])
    if routed:
        lane = lax.broadcasted_iota(jnp.int32, dg_s.shape, 1)
        u = u * jnp.sum(jnp.where(lane == e, dg_s[...], 0.0), axis=-1, keepdims=True)
    acc_s[...] += _dot(u.astype(BF16), w2_ref[0])

    @pl.when((e == pl.num_programs(1) - 1) & (f == pl.num_programs(2) - 1))
    def _():
        o_ref[...] = x_ref[...] + gate_ref[0] * _rms(acc_s[...], gpost_ref[...])


def swiglu_mixer(x, mod, g_pre, g_post, w1, w3, w2, router_w=None, router_b=None, *, tm, tf):
    routed = router_w is not None
    n_e, _, ff = w1.shape
    if not routed:
        router_w = jnp.zeros((D_MODEL, LANES), F32)
        router_b = jnp.zeros((1, LANES), F32)
    const = lambda shape: pl.BlockSpec(shape, lambda i, e, f: (0,) * len(shape))
    return pl.pallas_call(
        functools.partial(_swiglu_kernel, routed=routed),
        grid=(N_TOK // tm, n_e, ff // tf),
        in_specs=[pl.BlockSpec((tm, D_MODEL), lambda i, e, f: (i, 0)),
                  _mod_spec(3, tm), _mod_spec(4, tm), _mod_spec(5, tm),
                  const((1, D_MODEL)), const((1, D_MODEL)),
                  const((D_MODEL, LANES)), const((1, LANES)),
                  pl.BlockSpec((1, D_MODEL, tf), lambda i, e, f: (e, 0, f)),
                  pl.BlockSpec((1, D_MODEL, tf), lambda i, e, f: (e, 0, f)),
                  pl.BlockSpec((1, tf, D_MODEL), lambda i, e, f: (e, f, 0))],
        out_specs=pl.BlockSpec((tm, D_MODEL), lambda i, e, f: (i, 0)),
        out_shape=jax.ShapeDtypeStruct((N_TOK, D_MODEL), F32),
        scratch_shapes=[pltpu.VMEM((tm, D_MODEL), BF16),
                        pltpu.VMEM((tm, D_MODEL), F32),
                        pltpu.VMEM((tm, LANES), F32)],
        compiler_params=_cparams(3),
        name="swiglu_routed" if routed else "swiglu_dense",
    )(x, mod, mod, mod, g_pre.reshape(1, D_MODEL), g_post.reshape(1, D_MODEL), router_w, router_b, w1, w3, w2)


def _even_layer(x, mod, g, e, lam_init, w_in, w_out, a_lam, subln_g, rpb, cache_a_k, cache_a_v,
                cache_b_k, cache_b_v, ff_w1, ff_w3, ff_w2):
    proj = nm_matmul(x, mod, g[0], w_in.astype(BF16), tm=1024, tn=768)
    o_ctx = attn_ctx(proj, a_lam, subln_g, lam_init)
    o_a = attn_lat_a(proj, cache_a_k, cache_a_v, e, a_lam, subln_g, lam_init, tq=256)
    o_b = attn_lat_b(proj, cache_b_k, cache_b_v, e, rpb)
    mixed = jnp.concatenate([o_ctx, jnp.concatenate([o_a, o_b], axis=1)], axis=0)
    x = out_residual(mixed, w_out.astype(BF16), x, mod, g[1], tm=512)
    x = swiglu_mixer(x, mod, g[2], g[3], ff_w1.astype(BF16)[None], ff_w3.astype(BF16)[None],
                     ff_w2.astype(BF16)[None], tm=1024, tf=256)
    kv = proj[:N_CTX].reshape(BATCH, SEQ, IN_EVEN)
    new_kv = (kv[..., A_W:2 * A_W].reshape(BATCH, SEQ, A_HEADS, 2 * A_HD),
              kv[..., 2 * A_W:3 * A_W].reshape(BATCH, SEQ, A_HEADS, 2 * A_HD),
              kv[..., 3 * A_W + B_W:3 * A_W + 2 * B_W].reshape(BATCH, SEQ, B_HEADS, B_HD),
              kv[..., 3 * A_W + 2 * B_W:].reshape(BATCH, SEQ, B_HEADS, B_HD))
    return x, new_kv


def _odd_w_in(w):
    g0 = 4 * C_W
    g1 = g0 + 4 * C_HEADS
    pad = jnp.zeros((D_MODEL, LANES - 4 * C_HEADS), w.dtype)
    return jnp.concatenate([w[:, :g0], w[:, g1:], w[:, g0:g1], pad], axis=1)


def _odd_layer(x, mod, g, j, w_in, gate_b, conv_c, conv_d, conv_d_b, rg_wa, rg_ba, rg_wx, rg_bx, rg_lam,
               c_norm_g, w_out, state_c_C, state_c_n, state_c_m, state_d_h,
               router_w, router_b, moe_w1, moe_w3, moe_w2):
    proj = nm_matmul(x, mod, g[0], _odd_w_in(w_in).astype(BF16), tm=1024, tn=640)
    gates = proj[:, IN_ODD_PAD - LANES:IN_ODD_PAD - LANES + 4 * C_HEADS]
    gates_t = gates.reshape(N_TOK // C_CHUNK, C_CHUNK, 4 * C_HEADS).transpose(0, 2, 1)
    nh = 2 * C_HEADS
    nc_ctx = N_CTX // C_CHUNK
    zc = (jnp.zeros((BATCH, nh, C_DK, C_DV), F32), jnp.zeros((BATCH, nh, 1, C_DK), F32),
          jnp.zeros((BATCH, nh, 1, LANES), F32))
    sc = (state_c_C.reshape(DEC_BATCH, nh, C_DK, C_DV), state_c_n.reshape(DEC_BATCH, nh, 1, C_DK),
          jnp.broadcast_to(state_c_m.reshape(DEC_BATCH, nh, 1, 1), (DEC_BATCH, nh, 1, LANES)))
    hc_p, cf, nf, mf = mlstm(proj, gates_t[:nc_ctx], gate_b, conv_c, c_norm_g, *zc,
                             seq=SEQ, nb=BATCH, row0=0)
    hc_s, _, _, _ = mlstm(proj, gates_t[nc_ctx:], gate_b, conv_c, c_norm_g, *sc,
                          seq=DEC_SEQ, nb=DEC_BATCH, row0=N_CTX)
    rg = (conv_d, conv_d_b, rg_wa, rg_ba, rg_wx, rg_bx, rg_lam)
    hd_p, hfin = rglru(proj, *rg, jnp.zeros((BATCH, 2, 1, D_RNN), F32), seq=SEQ, nb=BATCH, row0=0)
    hd_s, _ = rglru(proj, *rg, state_d_h.reshape(DEC_BATCH, 2, 1, D_RNN),
                    seq=DEC_SEQ, nb=DEC_BATCH, row0=N_CTX)
    mixed = jnp.concatenate([jnp.concatenate([hc_p, hd_p], axis=1),
                             jnp.concatenate([hc_s, hd_s], axis=1)], axis=0)
    x = out_residual(mixed, w_out.astype(BF16), x, mod, g[1], tm=512)
    rw = jnp.zeros((D_MODEL, LANES), F32).at[:, :N_EXPERTS].set(router_w)
    rb = jnp.zeros((1, LANES), F32).at[0, :N_EXPERTS].set(router_b)
    x = swiglu_mixer(x, mod, g[2], g[3], moe_w1.astype(BF16), moe_w3.astype(BF16), moe_w2.astype(BF16),
                     rw, rb, tm=1024, tf=512)
    states = (cf.reshape(BATCH, 2, C_HEADS, C_DK, C_DV), nf.reshape(BATCH, 2, C_HEADS, C_DK),
              mf[..., 0].reshape(BATCH, 2, C_HEADS), hfin.reshape(BATCH, 2, D_RNN))
    return x, states


def kernel(x_prompt, x_sample, c, cache_a_k, cache_a_v, cache_b_k, cache_b_v, state_c_C, state_c_n, state_c_m, state_d_h, c_ctx, mod_w, mod_b, norm_g, w_in_even, w_out_even, a_lam, a_subln_g, b_rpb, ff_w1, ff_w3, ff_w2, w_in_odd, c_gate_b, conv_c, conv_d, conv_d_b, rg_wa, rg_ba, rg_wx, rg_bx, rg_lam, c_norm_g, w_out_odd, router_w, router_b, moe_w1, moe_w3, moe_w2):
    x = jnp.concatenate([x_prompt.reshape(N_CTX, D_MODEL), x_sample.reshape(N_LAT, D_MODEL)], axis=0)
    cvec = jnp.concatenate([c_ctx[None], c, jnp.zeros((MOD_ROWS - 1 - DEC_BATCH, D_MODEL), F32)], axis=0)
    mods = adaln_all(cvec, mod_w, mod_b)
    ca_k = cache_a_k.reshape(DEC_BATCH, N_EVEN, PAST_LEN, A_W)
    ca_v = cache_a_v.reshape(DEC_BATCH, N_EVEN, PAST_LEN, A_W)
    cb_k = cache_b_k.reshape(DEC_BATCH, N_EVEN, PAST_LEN, B_W)
    cb_v = cache_b_v.reshape(DEC_BATCH, N_EVEN, PAST_LEN, B_W)
    new_kv = []
    new_st = []
    for l in range(DEPTH):
        if l % 2 == 0:
            e = l // 2
            lam_init = 0.8 - 0.6 * math.exp(-0.3 * l)
            x, kv = _even_layer(x, mods[l], norm_g[l], e, lam_init, w_in_even[e], w_out_even[e], a_lam[e],
                                a_subln_g[e], b_rpb[e], ca_k, ca_v, cb_k, cb_v, ff_w1[e], ff_w3[e], ff_w2[e])
            new_kv.append(kv)
        else:
            j = l // 2
            x, st = _odd_layer(x, mods[l], norm_g[l], j, w_in_odd[j], c_gate_b[j], conv_c[j], conv_d[j],
                               conv_d_b[j], rg_wa[j], rg_ba[j], rg_wx[j], rg_bx[j], rg_lam[j], c_norm_g[j],
                               w_out_odd[j], state_c_C[:, j], state_c_n[:, j], state_c_m[:, j], state_d_h[:, j],
                               router_w[j], router_b[j], moe_w1[j], moe_w3[j], moe_w2[j])
            new_st.append(st)
    y_p = x[:N_CTX].reshape(BATCH, SEQ, D_MODEL)
    y_s = x[N_CTX:].reshape(DEC_BATCH, DEC_SEQ, D_MODEL)
    kv_out = tuple(jnp.stack([kv[i] for kv in new_kv], axis=1) for i in range(4))
    st_out = tuple(jnp.stack([st[i] for st in new_st], axis=1) for i in range(4))
    return (y_p, y_s) + kv_out + st_out
```

```python
import functools
import math

import numpy as np
import jax
import jax.numpy as jnp
from jax import lax
from jax.experimental import pallas as pl
from jax.experimental.pallas import tpu as pltpu

D_MODEL = 1024
BATCH = 32
SEQ = 256
DEPTH = 4
DEC_BATCH = 4
DEC_SEQ = 1024
PAST_LEN = 256
GRID_W = 64
GRID_ROWS = DEC_SEQ // GRID_W
N_EVEN = (DEPTH + 1) // 2
N_ODD = DEPTH // 2
A_HEADS = 4
A_HD = 64
B_HEADS = 8
B_HD = 64
NA_ROWS = 8
NA_COLS = 16
C_HEADS = 4
C_DK = 128
C_DV = 128
C_CHUNK = 64
D_RNN = 512
D_BLOCKS = 8
D_BW = D_RNN // D_BLOCKS
RG_C = 8.0
CONV_W = 4
FF_DENSE = 2816
N_EXPERTS = 8
FF_EXPERT = 2048
ROPE_BASE = 10000.0
EPS = 1e-6
NEG_INF = -1e30
A_W = A_HEADS * 2 * A_HD
B_W = B_HEADS * B_HD
C_W = C_HEADS * C_DK
IN_EVEN = 3 * A_W + 3 * B_W
IN_ODD_PAD = 4 * C_W + 2 * D_RNN + 128
F32 = jnp.float32
BF16 = jnp.bfloat16

N_CTX = BATCH * SEQ
N_LAT = DEC_BATCH * DEC_SEQ
N_TOK = N_CTX + N_LAT
MOD_ROWS = 8
LANES = 128
VMEM_LIMIT = 56 * 1024 * 1024
HI = lax.Precision.HIGHEST


def _cparams(n_axes):
    return pltpu.CompilerParams(dimension_semantics=("arbitrary",) * n_axes,
                                vmem_limit_bytes=VMEM_LIMIT)


def _sigmoid(x):
    return 1.0 / (1.0 + jnp.exp(-x))


def _silu(x):
    return x * _sigmoid(x)


def _softplus(x):
    return jnp.maximum(x, 0.0) + jnp.log1p(jnp.exp(-jnp.abs(x)))


def _gelu_tanh(x):
    return 0.5 * x * (1.0 + jnp.tanh(math.sqrt(2.0 / math.pi) * (x + 0.044715 * (x * x * x))))


def _rms(x, g):
    return x * lax.rsqrt(jnp.mean(x * x, axis=-1, keepdims=True) + EPS) * g


def _norm_mod(x, g, scale, shift):
    return _rms(x, g) * (1.0 + scale) + shift


def _dot(a, b):
    return jnp.dot(a, b, preferred_element_type=F32)


def _dot_nt(a, b):
    return lax.dot_general(a, b, (((1,), (1,)), ((), ())), preferred_element_type=F32)


def _dot_tn(a, b):
    return lax.dot_general(a, b, (((0,), (0,)), ((), ())), preferred_element_type=F32)


def _softmax_rows(s):
    e = jnp.exp(s - jnp.max(s, axis=-1, keepdims=True))
    return e / jnp.sum(e, axis=-1, keepdims=True)


def _mod_row(i, tm):
    r0 = i * tm
    return jnp.where(r0 < N_CTX, 0, 1 + (r0 - N_CTX) // DEC_SEQ)


def _mod_spec(which, tm):
    return pl.BlockSpec((1, 1, D_MODEL), lambda i, *_: (which * MOD_ROWS + _mod_row(i, tm), 0, 0))


def _adaln_kernel(c_ref, w_ref, b_ref, o_ref):
    s = _silu(c_ref[...]).astype(BF16)
    o_ref[0] = _dot(s, w_ref[0].astype(BF16)) + b_ref[0]


def adaln_all(cvec, mod_w, mod_b):
    tn = 1536
    out = pl.pallas_call(
        _adaln_kernel,
        grid=(DEPTH, 6 * D_MODEL // tn),
        in_specs=[pl.BlockSpec((MOD_ROWS, D_MODEL), lambda l, j: (0, 0)),
                  pl.BlockSpec((1, D_MODEL, tn), lambda l, j: (l, 0, j)),
                  pl.BlockSpec((1, 1, tn), lambda l, j: (l, 0, j))],
        out_specs=pl.BlockSpec((1, MOD_ROWS, tn), lambda l, j: (l, 0, j)),
        out_shape=jax.ShapeDtypeStruct((DEPTH, MOD_ROWS, 6 * D_MODEL), F32),
        compiler_params=_cparams(2),
        name="adaln",
    )(cvec, mod_w, mod_b.reshape(DEPTH, 1, 6 * D_MODEL))
    out = out.reshape(DEPTH, MOD_ROWS, 6, D_MODEL).transpose(0, 2, 1, 3)
    return out.reshape(DEPTH, 6 * MOD_ROWS, 1, D_MODEL)


def _nm_matmul_kernel(x_ref, sh_ref, sc_ref, g_ref, w_ref, o_ref, h_ref):
    @pl.when(pl.program_id(1) == 0)
    def _():
        h_ref[...] = _norm_mod(x_ref[...], g_ref[...], sc_ref[0], sh_ref[0]).astype(BF16)

    o_ref[...] = _dot(h_ref[...], w_ref[...].astype(BF16))


def nm_matmul(x, mod, g_pre, w, *, tm, tn):
    n = w.shape[1]
    return pl.pallas_call(
        _nm_matmul_kernel,
        grid=(N_TOK // tm, n // tn),
        in_specs=[pl.BlockSpec((tm, D_MODEL), lambda i, j: (i, 0)),
                  _mod_spec(0, tm), _mod_spec(1, tm),
                  pl.BlockSpec((1, D_MODEL), lambda i, j: (0, 0)),
                  pl.BlockSpec((D_MODEL, tn), lambda i, j: (0, j))],
        out_specs=pl.BlockSpec((tm, tn), lambda i, j: (i, j)),
        out_shape=jax.ShapeDtypeStruct((N_TOK, n), F32),
        scratch_shapes=[pltpu.VMEM((tm, D_MODEL), BF16)],
        compiler_params=_cparams(2),
        name="nm_matmul",
    )(x, mod, mod, g_pre.reshape(1, D_MODEL), w)


def _out_res_kernel(a_ref, w_ref, x_ref, gate_ref, g_ref, o_ref):
    out = _dot(a_ref[...].astype(BF16), w_ref[...].astype(BF16))
    o_ref[...] = x_ref[...] + gate_ref[0] * _rms(out, g_ref[...])


def out_residual(a, w, x, mod, g_post, *, tm):
    k = a.shape[1]
    return pl.pallas_call(
        _out_res_kernel,
        grid=(N_TOK // tm,),
        in_specs=[pl.BlockSpec((tm, k), lambda i: (i, 0)),
                  pl.BlockSpec((k, D_MODEL), lambda i: (0, 0)),
                  pl.BlockSpec((tm, D_MODEL), lambda i: (i, 0)),
                  _mod_spec(2, tm),
                  pl.BlockSpec((1, D_MODEL), lambda i: (0, 0))],
        out_specs=pl.BlockSpec((tm, D_MODEL), lambda i: (i, 0)),
        out_shape=jax.ShapeDtypeStruct((N_TOK, D_MODEL), F32),
        compiler_params=_cparams(1),
        name="out_residual",
    )(a, w, x, mod, g_post.reshape(1, D_MODEL))


def _lam_value(lam_ref, lam_init):
    lv = lam_ref[...]
    a = jnp.exp(jnp.sum(lv[0:1] * lv[1:2], axis=-1, keepdims=True))
    b = jnp.exp(jnp.sum(lv[2:3] * lv[3:4], axis=-1, keepdims=True))
    return a - b + lam_init


def _diff_attention(q, k, v, lam, subln_g, lam_init):
    outs = []
    for h in range(A_HEADS):
        ps = []
        for m in range(2):
            lo = (2 * h + m) * A_HD
            s = _dot_nt(q[:, lo:lo + A_HD], k[:, lo:lo + A_HD]) * (A_HD ** -0.5)
            ps.append(_softmax_rows(s))
        p = ps[0] - lam * ps[1]
        o = _dot(p.astype(BF16), v[:, 2 * A_HD * h:2 * A_HD * (h + 1)])
        outs.append(_rms(o, subln_g) * (1.0 - lam_init))
    return outs


def _attn_ctx_kernel(qa_ref, ka_ref, va_ref, qb_ref, kb_ref, vb_ref, lam_ref, sg_ref, o_ref, *, lam_init):
    lam = _lam_value(lam_ref, lam_init)
    outs = _diff_attention(qa_ref[...].astype(BF16), ka_ref[...].astype(BF16), va_ref[...].astype(BF16),
                           lam, sg_ref[...], lam_init)
    for h in range(A_HEADS):
        o_ref[:, 2 * A_HD * h:2 * A_HD * (h + 1)] = outs[h]
    qb = qb_ref[...].astype(BF16)
    kb = kb_ref[...].astype(BF16)
    vb = vb_ref[...].astype(BF16)
    for h in range(B_HEADS):
        sl = slice(B_HD * h, B_HD * (h + 1))
        p = _softmax_rows(_dot_nt(qb[:, sl], kb[:, sl]) * (B_HD ** -0.5))
        o_ref[:, A_W + B_HD * h:A_W + B_HD * (h + 1)] = _dot(p.astype(BF16), vb[:, sl])


def attn_ctx(proj, a_lam, subln_g, lam_init):
    specs = [pl.BlockSpec((SEQ, A_W), lambda b, c=c: (b, c)) for c in range(6)]
    return pl.pallas_call(
        functools.partial(_attn_ctx_kernel, lam_init=lam_init),
        grid=(BATCH,),
        in_specs=specs + [pl.BlockSpec((4, A_HD), lambda b: (0, 0)),
                          pl.BlockSpec((1, 2 * A_HD), lambda b: (0, 0))],
        out_specs=pl.BlockSpec((SEQ, A_W + B_W), lambda b: (b, 0)),
        out_shape=jax.ShapeDtypeStruct((N_CTX, A_W + B_W), F32),
        compiler_params=_cparams(1),
        name="attn_ctx",
    )(proj, proj, proj, proj, proj, proj, a_lam, subln_g.reshape(1, 2 * A_HD))


def _rope_tables():
    half = A_HD // 2
    t = np.arange(DEC_SEQ)
    pos = np.stack([t // GRID_W, t % GRID_W], axis=-1).astype(np.float32)
    freqs = (np.float32(ROPE_BASE) ** (-np.arange(0, half, 2, dtype=np.float32) / np.float32(half))).astype(np.float32)
    ang = (pos[:, :, None] * freqs).astype(np.float32)
    cos = np.cos(ang).astype(np.float32)
    sin = np.sin(ang).astype(np.float32)
    cos_h = np.concatenate([cos[:, 0], cos[:, 0], cos[:, 1], cos[:, 1]], axis=-1)
    sin_h = np.concatenate([-sin[:, 0], sin[:, 0], -sin[:, 1], sin[:, 1]], axis=-1)
    reps = A_W // A_HD
    return np.tile(cos_h, (1, reps)), np.tile(sin_h, (1, reps))


def _rope(x, cos, sin):
    quarter = A_HD // 4
    cols = []
    for c in range(x.shape[1] // LANES):
        xs = x[:, c * LANES:(c + 1) * LANES]
        up = pltpu.roll(xs, LANES - quarter, 1)
        dn = pltpu.roll(xs, quarter, 1)
        lane = lax.broadcasted_iota(jnp.int32, xs.shape, 1)
        cols.append(jnp.where((lane % (2 * quarter)) < quarter, up, dn))
    partner = jnp.concatenate(cols, axis=1)
    return x * cos + partner * sin


def _attn_lat_a_kernel(q_ref, k_ref, v_ref, ck_ref, cv_ref, cosq_ref, sinq_ref, cosk_ref, sink_ref,
                       lam_ref, sg_ref, o_ref, k_s, v_s, *, lam_init):
    @pl.when(pl.program_id(1) == 0)
    def _():
        k_s[0:DEC_SEQ, :] = _rope(k_ref[...], cosk_ref[...], sink_ref[...]).astype(BF16)
        k_s[DEC_SEQ:DEC_SEQ + PAST_LEN, :] = ck_ref[...].astype(BF16)
        v_s[0:DEC_SEQ, :] = v_ref[...].astype(BF16)
        v_s[DEC_SEQ:DEC_SEQ + PAST_LEN, :] = cv_ref[...].astype(BF16)

    lam = _lam_value(lam_ref, lam_init)
    q = _rope(q_ref[...], cosq_ref[...], sinq_ref[...]).astype(BF16)
    outs = _diff_attention(q, k_s[...], v_s[...], lam, sg_ref[...], lam_init)
    for h in range(A_HEADS):
        o_ref[:, 2 * A_HD * h:2 * A_HD * (h + 1)] = outs[h]


def attn_lat_a(proj, cache_k, cache_v, e, a_lam, subln_g, lam_init, *, tq):
    cos, sin = _rope_tables()
    nq = DEC_SEQ // tq
    q0 = N_CTX // tq
    b0 = N_CTX // DEC_SEQ
    cache_spec = pl.BlockSpec((None, None, PAST_LEN, A_W), lambda b, i: (b, e, 0, 0))
    return pl.pallas_call(
        functools.partial(_attn_lat_a_kernel, lam_init=lam_init),
        grid=(DEC_BATCH, nq),
        in_specs=[pl.BlockSpec((tq, A_W), lambda b, i: (q0 + b * nq + i, 0)),
                  pl.BlockSpec((DEC_SEQ, A_W), lambda b, i: (b0 + b, 1)),
                  pl.BlockSpec((DEC_SEQ, A_W), lambda b, i: (b0 + b, 2)),
                  cache_spec, cache_spec,
                  pl.BlockSpec((tq, A_W), lambda b, i: (i, 0)),
                  pl.BlockSpec((tq, A_W), lambda b, i: (i, 0)),
                  pl.BlockSpec((DEC_SEQ, A_W), lambda b, i: (0, 0)),
                  pl.BlockSpec((DEC_SEQ, A_W), lambda b, i: (0, 0)),
                  pl.BlockSpec((4, A_HD), lambda b, i: (0, 0)),
                  pl.BlockSpec((1, 2 * A_HD), lambda b, i: (0, 0))],
        out_specs=pl.BlockSpec((tq, A_W), lambda b, i: (b * nq + i, 0)),
        out_shape=jax.ShapeDtypeStruct((N_LAT, A_W), F32),
        scratch_shapes=[pltpu.VMEM((DEC_SEQ + PAST_LEN, A_W), BF16),
                        pltpu.VMEM((DEC_SEQ + PAST_LEN, A_W), BF16)],
        compiler_params=_cparams(2),
        name="attn_lat_a",
    )(proj, proj, proj, cache_k, cache_v, cos, sin, cos, sin, a_lam, subln_g.reshape(1, 2 * A_HD))


NA_WIN = NA_ROWS * GRID_W


def _na_bias_kernel(rpb_ref, rsel_ref, csel_ref, valid_ref, o_ref):
    rows = jnp.dot(rsel_ref[...], rpb_ref[0], precision=HI, preferred_element_type=F32)
    full = jnp.dot(rows, csel_ref[...], precision=HI, preferred_element_type=F32)
    o_ref[0] = jnp.where(valid_ref[...] > 0.5, full, NEG_INF)


def _na_bias(rpb):
    n_dr = 2 * NA_ROWS
    n_dc = 2 * NA_COLS
    r = np.arange(GRID_ROWS)
    r0 = np.clip(r - NA_ROWS // 2, 0, GRID_ROWS - NA_ROWS)
    krow = r0[:, None] + np.arange(NA_ROWS)
    dr = (krow - r[:, None] + NA_ROWS - 1).reshape(-1)
    qcol = np.arange(GRID_W)
    kcol = np.arange(GRID_W)
    cstart = np.clip(qcol - NA_COLS // 2, 0, GRID_W - NA_COLS)
    valid = (kcol[None, :] >= cstart[:, None]) & (kcol[None, :] < cstart[:, None] + NA_COLS)
    dc = np.clip(kcol[None, :] - qcol[:, None] + NA_COLS - 1, 0, 2 * NA_COLS - 2).reshape(-1)
    rsel = (dr[:, None] == np.arange(n_dr)[None, :]).astype(np.float32)
    csel = (np.arange(n_dc)[:, None] == dc[None, :]).astype(np.float32)
    rpb_p = jnp.pad(rpb.astype(F32), ((0, 0), (0, 1), (0, 1)))
    nrk = GRID_ROWS * NA_ROWS
    nqk = GRID_W * GRID_W
    full = lambda shape: pl.BlockSpec(shape, lambda h: (0,) * len(shape))
    bias = pl.pallas_call(
        _na_bias_kernel,
        grid=(B_HEADS,),
        in_specs=[pl.BlockSpec((1, n_dr, n_dc), lambda h: (h, 0, 0)),
                  full((nrk, n_dr)), full((n_dc, nqk)), full((1, nqk))],
        out_specs=pl.BlockSpec((1, nrk, nqk), lambda h: (h, 0, 0)),
        out_shape=jax.ShapeDtypeStruct((B_HEADS, nrk, nqk), F32),
        compiler_params=_cparams(1),
        name="na_bias",
    )(rpb_p, rsel, csel, valid.reshape(1, nqk).astype(np.float32))
    bias = bias.reshape(B_HEADS, GRID_ROWS, NA_ROWS, GRID_W, GRID_W).transpose(0, 1, 3, 2, 4)
    return bias.reshape(B_HEADS, GRID_ROWS, GRID_W, NA_WIN)


def _attn_lat_b_kernel(q_ref, k_ref, v_ref, ck_ref, cv_ref, bias_ref, o_ref):
    r = pl.program_id(1)
    r0 = jnp.clip(r - NA_ROWS // 2, 0, GRID_ROWS - NA_ROWS)
    win = pl.ds(pl.multiple_of(r0 * GRID_W, GRID_W), NA_WIN)
    q = q_ref[...].astype(BF16)
    kw = k_ref[win, :].astype(BF16)
    vw = v_ref[win, :].astype(BF16)
    kc = ck_ref[...].astype(BF16)
    vc = cv_ref[...].astype(BF16)
    scale = B_HD ** -0.5
    for h in range(B_HEADS):
        sl = slice(B_HD * h, B_HD * (h + 1))
        s_win = _dot_nt(q[:, sl], kw[:, sl]) * scale + bias_ref[h, 0]
        s_ctx = _dot_nt(q[:, sl], kc[:, sl]) * scale
        m = jnp.maximum(jnp.max(s_win, axis=-1, keepdims=True), jnp.max(s_ctx, axis=-1, keepdims=True))
        e_win = jnp.exp(s_win - m)
        e_ctx = jnp.exp(s_ctx - m)
        den = jnp.sum(e_win, axis=-1, keepdims=True) + jnp.sum(e_ctx, axis=-1, keepdims=True)
        p_win = (e_win / den).astype(BF16)
        p_ctx = (e_ctx / den).astype(BF16)
        o_ref[:, sl] = _dot(p_win, vw[:, sl]) + _dot(p_ctx, vc[:, sl])


def attn_lat_b(proj, cache_k, cache_v, e, rpb):
    bias = _na_bias(rpb)
    q0 = N_CTX // GRID_W
    b0 = N_CTX // DEC_SEQ
    cache_spec = pl.BlockSpec((None, None, PAST_LEN, B_W), lambda b, r: (b, e, 0, 0))
    return pl.pallas_call(
        _attn_lat_b_kernel,
        grid=(DEC_BATCH, GRID_ROWS),
        in_specs=[pl.BlockSpec((GRID_W, B_W), lambda b, r: (q0 + b * GRID_ROWS + r, 3)),
                  pl.BlockSpec((DEC_SEQ, B_W), lambda b, r: (b0 + b, 4)),
                  pl.BlockSpec((DEC_SEQ, B_W), lambda b, r: (b0 + b, 5)),
                  cache_spec, cache_spec,
                  pl.BlockSpec((B_HEADS, 1, GRID_W, NA_WIN), lambda b, r: (0, r, 0, 0))],
        out_specs=pl.BlockSpec((GRID_W, B_W), lambda b, r: (b * GRID_ROWS + r, 0)),
        out_shape=jax.ShapeDtypeStruct((N_LAT, B_W), F32),
        compiler_params=_cparams(2),
        name="attn_lat_b",
    )(proj, proj, proj, cache_k, cache_v, bias)


CONV_PAD = 8


def _conv_centred(x_ref, pad_ref, w_ref, seq):
    c = pad_ref.shape[1]
    pad_ref[0:CONV_PAD, :] = jnp.zeros((CONV_PAD, c), F32)
    pad_ref[CONV_PAD + seq:2 * CONV_PAD + seq, :] = jnp.zeros((CONV_PAD, c), F32)
    pad_ref[CONV_PAD:CONV_PAD + seq, :] = x_ref[...]
    left = (CONV_W - 1) // 2
    acc = None
    for j in range(CONV_W):
        start = CONV_PAD - left + j
        term = w_ref[j:j + 1, :] * pad_ref[start:start + seq, :]
        acc = term if acc is None else acc + term
    return acc


def _mlstm_kernel(qk_ref, v_ref, o_ref, g_ref, gt_ref, gb_ref, gbt_ref, cw_ref, ng_ref,
                  c0_ref, n0_ref, m0_ref, out_ref, cf_ref, nf_ref, mf_ref,
                  pad_s, qk_s, acc_s, c_s, n_s, m_s, *, seq):
    L = C_CHUNK
    nc = seq // L
    qk_s[...] = _silu(_conv_centred(qk_ref, pad_s, cw_ref, seq))
    row = lax.broadcasted_iota(jnp.int32, (L, L), 0)
    col = lax.broadcasted_iota(jnp.int32, (L, L), 1)
    gb = gb_ref[...]
    gbt = gbt_ref[...]
    ng = ng_ref[...]
    for d in range(2):
        tri = (col <= row) if d == 0 else (col >= row)
        tri_f = tri.astype(F32)
        tri_t = ((row <= col) if d == 0 else (row >= col)).astype(F32)
        last = L - 1 if d == 0 else 0
        for h in range(C_HEADS):
            c_s[h] = c0_ref[d * C_HEADS + h]
        n_s[...] = n0_ref[d * C_HEADS:(d + 1) * C_HEADS]
        m_s[...] = m0_ref[d * C_HEADS:(d + 1) * C_HEADS]

        def chunk_step(ci, carry, d=d, tri=tri, tri_f=tri_f, tri_t=tri_t, last=last):
            c_idx = ci if d == 0 else nc - 1 - ci
            rows = pl.ds(pl.multiple_of(c_idx * L, L), L)
            g = g_ref[rows, :] + gb
            gt = gt_ref[c_idx] + gbt
            ls = -_softplus(-g)
            lst = -_softplus(-gt)
            bcol = jnp.dot(tri_f, ls, precision=HI, preferred_element_type=F32)
            brow = jnp.dot(lst, tri_t, precision=HI, preferred_element_type=F32)
            for h in range(C_HEADS):
                ci_ = d * 2 * C_HEADS + h
                cf_ = ci_ + C_HEADS
                b_c = bcol[:, cf_:cf_ + 1]
                b_r = brow[cf_:cf_ + 1, :]
                i_c = g[:, ci_:ci_ + 1]
                i_r = gt[ci_:ci_ + 1, :]
                m_prev = m_s[h][:, 0:1]
                n_prev = n_s[h]
                c_prev = c_s[h]
                dlog = jnp.where(tri, b_c - b_r + i_r, -jnp.inf)
                inter = b_c + m_prev
                m_t = jnp.maximum(inter, jnp.max(dlog, axis=-1, keepdims=True))
                w_in = jnp.exp(dlog - m_t)
                w_st = jnp.exp(inter - m_t)
                q = qk_s[rows, C_DK * h:C_DK * (h + 1)] * (C_DK ** -0.5)
                k = qk_s[rows, C_W + C_DK * h:C_W + C_DK * (h + 1)]
                v = v_ref[rows, C_DV * h:C_DV * (h + 1)]
                qb = q.astype(BF16)
                kb = k.astype(BF16)
                vb = v.astype(BF16)
                s = _dot_nt(qb, kb) * w_in
                num = w_st * _dot(qb, c_prev.astype(BF16)) + _dot(s.astype(BF16), vb)
                nq = w_st * jnp.sum(q * n_prev, axis=-1, keepdims=True) + jnp.sum(s, axis=-1, keepdims=True)
                hh = num / jnp.maximum(jnp.abs(nq), jnp.exp(-m_t))
                b_last = b_c[last:last + 1, :]
                wlog = b_last - b_c + i_c
                m_new = jnp.maximum(b_last + m_prev, jnp.max(wlog, axis=0, keepdims=True))
                ws = jnp.exp(wlog - m_new)
                wc = jnp.exp(b_last + m_prev - m_new)
                kw = ws * k
                c_s[h] = wc * c_prev + _dot_tn(kw.astype(BF16), vb)
                n_s[h] = wc * n_prev + jnp.sum(kw, axis=0, keepdims=True)
                m_s[h] = jnp.broadcast_to(m_new, (1, LANES))
                hs = slice(C_DV * h, C_DV * (h + 1))
                if d == 0:
                    acc_s[rows, hs] = hh
                else:
                    tot = acc_s[rows, hs] + hh
                    out_ref[rows, hs] = _rms(tot, ng) * _sigmoid(o_ref[rows, hs])
            return carry

        lax.fori_loop(0, nc, chunk_step, 0)
        for h in range(C_HEADS):
            cf_ref[d * C_HEADS + h] = c_s[h]
        nf_ref[d * C_HEADS:(d + 1) * C_HEADS] = n_s[...]
        mf_ref[d * C_HEADS:(d + 1) * C_HEADS] = m_s[...]


def mlstm(proj, gates_t, gate_b, conv_c, norm_g, c0, n0, m0, *, seq, nb, row0):
    nh = 2 * C_HEADS
    b0 = row0 // seq
    nc = seq // C_CHUNK
    gate_cols = (IN_ODD_PAD - LANES) // LANES
    gb = jnp.zeros((1, LANES), F32).at[0, :4 * C_HEADS].set(gate_b)
    out_shapes = (jax.ShapeDtypeStruct((nb * seq, C_W), F32),
                  jax.ShapeDtypeStruct((nb, nh, C_DK, C_DV), F32),
                  jax.ShapeDtypeStruct((nb, nh, 1, C_DK), F32),
                  jax.ShapeDtypeStruct((nb, nh, 1, LANES), F32))
    state_specs = [pl.BlockSpec((None, nh, C_DK, C_DV), lambda b: (b, 0, 0, 0)),
                   pl.BlockSpec((None, nh, 1, C_DK), lambda b: (b, 0, 0, 0)),
                   pl.BlockSpec((None, nh, 1, LANES), lambda b: (b, 0, 0, 0))]
    return pl.pallas_call(
        functools.partial(_mlstm_kernel, seq=seq),
        grid=(nb,),
        in_specs=[pl.BlockSpec((seq, 2 * C_W), lambda b: (b0 + b, 0)),
                  pl.BlockSpec((seq, C_W), lambda b: (b0 + b, 2)),
                  pl.BlockSpec((seq, C_W), lambda b: (b0 + b, 3)),
                  pl.BlockSpec((seq, LANES), lambda b: (b0 + b, gate_cols)),
                  pl.BlockSpec((nc, 4 * C_HEADS, C_CHUNK), lambda b: (b, 0, 0)),
                  pl.BlockSpec((1, LANES), lambda b: (0, 0)),
                  pl.BlockSpec((4 * C_HEADS, 1), lambda b: (0, 0)),
                  pl.BlockSpec((CONV_W, 2 * C_W), lambda b: (0, 0)),
                  pl.BlockSpec((1, C_DV), lambda b: (0, 0))] + state_specs,
        out_specs=[pl.BlockSpec((seq, C_W), lambda b: (b, 0))] + state_specs,
        out_shape=out_shapes,
        scratch_shapes=[pltpu.VMEM((seq + 2 * CONV_PAD, 2 * C_W), F32),
                        pltpu.VMEM((seq, 2 * C_W), F32),
                        pltpu.VMEM((seq, C_W), F32),
                        pltpu.VMEM((C_HEADS, C_DK, C_DV), F32),
                        pltpu.VMEM((C_HEADS, 1, C_DK), F32),
                        pltpu.VMEM((C_HEADS, 1, LANES), F32)],
        compiler_params=_cparams(1),
        name="mlstm",
    )(proj, proj, proj, proj, gates_t, gb, gate_b.reshape(4 * C_HEADS, 1), conv_c,
      norm_g.reshape(1, C_DV), c0, n0, m0)


RG_BLOCK = 64


def _rglru_kernel(xd_ref, gd_ref, cw_ref, cb_ref, wa_ref, wx_ref, ba_ref, bx_ref, lam_ref, h0_ref,
                  out_ref, hf_ref, pad_s, xc_s, acc_s, *, seq):
    R = RG_BLOCK
    nblk = seq // R
    xc_s[...] = _conv_centred(xd_ref, pad_s, cw_ref, seq) + cb_ref[...]
    row = lax.broadcasted_iota(jnp.int32, (R, D_RNN), 0)
    for d in range(2):
        sp = _softplus(-lam_ref[d])
        wa = wa_ref[d]
        wx = wx_ref[d]
        ba = ba_ref[d]
        bx = bx_ref[d]

        def block_step(bi, carry, d=d, sp=sp, wa=wa, wx=wx, ba=ba, bx=bx):
            b_idx = bi if d == 0 else nblk - 1 - bi
            rows = pl.ds(pl.multiple_of(b_idx * R, R), R)
            x = xc_s[rows, :]
            xb = x.astype(BF16)
            r = _sigmoid(_dot(xb, wa) + ba)
            ig = _sigmoid(_dot(xb, wx) + bx)
            log_a = -RG_C * r * sp
            a = jnp.exp(log_a)
            u = jnp.sqrt(-jnp.tanh(log_a) * (a * a + 1.0)) * (ig * x)
            sh = 1
            while sh < R:
                if d == 0:
                    a_s = pltpu.roll(a, sh, 0)
                    u_s = pltpu.roll(u, sh, 0)
                    ok = row >= sh
                else:
                    a_s = pltpu.roll(a, R - sh, 0)
                    u_s = pltpu.roll(u, R - sh, 0)
                    ok = row < R - sh
                u = u + a * jnp.where(ok, u_s, 0.0)
                a = a * jnp.where(ok, a_s, 1.0)
                sh *= 2
            hs = u + a * carry
            if d == 0:
                acc_s[rows, :] = hs
                return hs[R - 1:R, :]
            out_ref[rows, :] = (acc_s[rows, :] + hs) * _gelu_tanh(gd_ref[rows, :])
            return hs[0:1, :]

        hf_ref[d] = lax.fori_loop(0, nblk, block_step, h0_ref[d])


def _block_diag(w):
    eye = jnp.eye(D_BLOCKS, dtype=w.dtype)
    return (w[:, :, None, :] * eye[:, None, :, None]).reshape(D_RNN, D_RNN)


def rglru(proj, conv_d, conv_d_b, rg_wa, rg_ba, rg_wx, rg_bx, rg_lam, h0, *, seq, nb, row0):
    b0 = row0 // seq
    wa = jnp.stack([_block_diag(rg_wa[0]), _block_diag(rg_wa[1])]).astype(BF16)
    wx = jnp.stack([_block_diag(rg_wx[0]), _block_diag(rg_wx[1])]).astype(BF16)
    vec = lambda a: a.reshape(2, 1, D_RNN)
    full = lambda shape: pl.BlockSpec(shape, lambda b: (0,) * len(shape))
    return pl.pallas_call(
        functools.partial(_rglru_kernel, seq=seq),
        grid=(nb,),
        in_specs=[pl.BlockSpec((seq, D_RNN), lambda b: (b0 + b, 4)),
                  pl.BlockSpec((seq, D_RNN), lambda b: (b0 + b, 5)),
                  full((CONV_W, D_RNN)), full((1, D_RNN)),
                  full((2, D_RNN, D_RNN)), full((2, D_RNN, D_RNN)),
                  full((2, 1, D_RNN)), full((2, 1, D_RNN)), full((2, 1, D_RNN)),
                  pl.BlockSpec((None, 2, 1, D_RNN), lambda b: (b, 0, 0, 0))],
        out_specs=[pl.BlockSpec((seq, D_RNN), lambda b: (b, 0)),
                   pl.BlockSpec((None, 2, 1, D_RNN), lambda b: (b, 0, 0, 0))],
        out_shape=(jax.ShapeDtypeStruct((nb * seq, D_RNN), F32),
                   jax.ShapeDtypeStruct((nb, 2, 1, D_RNN), F32)),
        scratch_shapes=[pltpu.VMEM((seq + 2 * CONV_PAD, D_RNN), F32),
                        pltpu.VMEM((seq, D_RNN), F32),
                        pltpu.VMEM((seq, D_RNN), F32)],
        compiler_params=_cparams(1),
        name="rglru",
    )(proj, proj, conv_d, conv_d_b.reshape(1, D_RNN), wa, wx, vec(rg_ba), vec(rg_bx), vec(rg_lam), h0)


def _router_gates(h, rw_ref, rb_ref):
    logits = jnp.dot(h, rw_ref[...], precision=HI, preferred_element_type=F32) + rb_ref[...]
    lane = lax.broadcasted_iota(jnp.int32, logits.shape, 1)
    logits = jnp.where(lane < N_EXPERTS, logits, -jnp.inf)
    m1 = jnp.max(logits, axis=-1, keepdims=True)
    i1 = jnp.min(jnp.where(logits == m1, lane, LANES), axis=-1, keepdims=True)
    rest = jnp.where(lane == i1, -jnp.inf, logits)
    m2 = jnp.max(rest, axis=-1, keepdims=True)
    i2 = jnp.min(jnp.where(rest == m2, lane, LANES), axis=-1, keepdims=True)
    e2 = jnp.exp(m2 - m1)
    den = 1.0 + e2
    return jnp.where(lane == i1, 1.0 / den, 0.0) + jnp.where(lane == i2, e2 / den, 0.0)


def _swiglu_kernel(x_ref, sh_ref, sc_ref, gate_ref, gpre_ref, gpost_ref, rw_ref, rb_ref,
                   w1_ref, w3_ref, w2_ref, o_ref, h_s, acc_s, dg_s, *, routed):
    e = pl.program_id(1)
    f = pl.program_id(2)

    @pl.when((e == 0) & (f == 0))
    def _():
        h = _norm_mod(x_ref[...], gpre_ref[...], sc_ref[0], sh_ref[0])
        h_s[...] = h.astype(BF16)
        acc_s[...] = jnp.zeros_like(acc_s)
        if routed:
            dg_s[...] = _router_gates(h, rw_ref, rb_ref)

    h = h_s[...]
    u = _silu(_dot(h, w1_ref[0].astype(BF16))) * _dot(h, w3_ref[0].astype(BF16))
    if routed:
        lane = lax.broadcasted_iota(jnp.int32, dg_s.shape, 1)
        u = u * jnp.sum(jnp.where(lane == e, dg_s[...], 0.0), axis=-1, keepdims=True)
    acc_s[...] += _dot(u.astype(BF16), w2_ref[0].astype(BF16))

    @pl.when((e == pl.num_programs(1) - 1) & (f == pl.num_programs(2) - 1))
    def _():
        o_ref[...] = x_ref[...] + gate_ref[0] * _rms(acc_s[...], gpost_ref[...])


def swiglu_mixer(x, mod, g_pre, g_post, w1, w3, w2, router_w=None, router_b=None, *, tm, tf):
    routed = router_w is not None
    n_e, _, ff = w1.shape
    if not routed:
        router_w = jnp.zeros((D_MODEL, LANES), F32)
        router_b = jnp.zeros((1, LANES), F32)
    const = lambda shape: pl.BlockSpec(shape, lambda i, e, f: (0,) * len(shape))
    return pl.pallas_call(
        functools.partial(_swiglu_kernel, routed=routed),
        grid=(N_TOK // tm, n_e, ff // tf),
        in_specs=[pl.BlockSpec((tm, D_MODEL), lambda i, e, f: (i, 0)),
                  _mod_spec(3, tm), _mod_spec(4, tm), _mod_spec(5, tm),
                  const((1, D_MODEL)), const((1, D_MODEL)),
                  const((D_MODEL, LANES)), const((1, LANES)),
                  pl.BlockSpec((1, D_MODEL, tf), lambda i, e, f: (e, 0, f)),
                  pl.BlockSpec((1, D_MODEL, tf), lambda i, e, f: (e, 0, f)),
                  pl.BlockSpec((1, tf, D_MODEL), lambda i, e, f: (e, f, 0))],
        out_specs=pl.BlockSpec((tm, D_MODEL), lambda i, e, f: (i, 0)),
        out_shape=jax.ShapeDtypeStruct((N_TOK, D_MODEL), F32),
        scratch_shapes=[pltpu.VMEM((tm, D_MODEL), BF16),
                        pltpu.VMEM((tm, D_MODEL), F32),
                        pltpu.VMEM((tm, LANES), F32)],
        compiler_params=_cparams(3),
        name="swiglu_routed" if routed else "swiglu_dense",
    )(x, mod, mod, mod, g_pre.reshape(1, D_MODEL), g_post.reshape(1, D_MODEL), router_w, router_b, w1, w3, w2)


def _even_layer(x, mod, g, e, lam_init, w_in, w_out, a_lam, subln_g, rpb, cache_a_k, cache_a_v,
                cache_b_k, cache_b_v, ff_w1, ff_w3, ff_w2):
    proj = nm_matmul(x, mod, g[0], w_in, tm=1024, tn=768)
    o_ctx = attn_ctx(proj, a_lam, subln_g, lam_init)
    o_a = attn_lat_a(proj, cache_a_k, cache_a_v, e, a_lam, subln_g, lam_init, tq=256)
    o_b = attn_lat_b(proj, cache_b_k, cache_b_v, e, rpb)
    mixed = jnp.concatenate([o_ctx, jnp.concatenate([o_a, o_b], axis=1)], axis=0)
    x = out_residual(mixed, w_out, x, mod, g[1], tm=512)
    x = swiglu_mixer(x, mod, g[2], g[3], ff_w1[None], ff_w3[None], ff_w2[None], tm=1024, tf=256)
    kv = proj[:N_CTX].reshape(BATCH, SEQ, IN_EVEN)
    new_kv = (kv[..., A_W:2 * A_W].reshape(BATCH, SEQ, A_HEADS, 2 * A_HD),
              kv[..., 2 * A_W:3 * A_W].reshape(BATCH, SEQ, A_HEADS, 2 * A_HD),
              kv[..., 3 * A_W + B_W:3 * A_W + 2 * B_W].reshape(BATCH, SEQ, B_HEADS, B_HD),
              kv[..., 3 * A_W + 2 * B_W:].reshape(BATCH, SEQ, B_HEADS, B_HD))
    return x, new_kv


def _odd_w_in(w):
    g0 = 4 * C_W
    g1 = g0 + 4 * C_HEADS
    pad = jnp.zeros((D_MODEL, LANES - 4 * C_HEADS), w.dtype)
    return jnp.concatenate([w[:, :g0], w[:, g1:], w[:, g0:g1], pad], axis=1)


def _odd_layer(x, mod, g, j, w_in, gate_b, conv_c, conv_d, conv_d_b, rg_wa, rg_ba, rg_wx, rg_bx, rg_lam,
               c_norm_g, w_out, state_c_C, state_c_n, state_c_m, state_d_h,
               router_w, router_b, moe_w1, moe_w3, moe_w2):
    proj = nm_matmul(x, mod, g[0], _odd_w_in(w_in), tm=1024, tn=640)
    gates = proj[:, IN_ODD_PAD - LANES:IN_ODD_PAD - LANES + 4 * C_HEADS]
    gates_t = gates.reshape(N_TOK // C_CHUNK, C_CHUNK, 4 * C_HEADS).transpose(0, 2, 1)
    nh = 2 * C_HEADS
    nc_ctx = N_CTX // C_CHUNK
    zc = (jnp.zeros((BATCH, nh, C_DK, C_DV), F32), jnp.zeros((BATCH, nh, 1, C_DK), F32),
          jnp.zeros((BATCH, nh, 1, LANES), F32))
    sc = (state_c_C.reshape(DEC_BATCH, nh, C_DK, C_DV), state_c_n.reshape(DEC_BATCH, nh, 1, C_DK),
          jnp.broadcast_to(state_c_m.reshape(DEC_BATCH, nh, 1, 1), (DEC_BATCH, nh, 1, LANES)))
    hc_p, cf, nf, mf = mlstm(proj, gates_t[:nc_ctx], gate_b, conv_c, c_norm_g, *zc,
                             seq=SEQ, nb=BATCH, row0=0)
    hc_s, _, _, _ = mlstm(proj, gates_t[nc_ctx:], gate_b, conv_c, c_norm_g, *sc,
                          seq=DEC_SEQ, nb=DEC_BATCH, row0=N_CTX)
    rg = (conv_d, conv_d_b, rg_wa, rg_ba, rg_wx, rg_bx, rg_lam)
    hd_p, hfin = rglru(proj, *rg, jnp.zeros((BATCH, 2, 1, D_RNN), F32), seq=SEQ, nb=BATCH, row0=0)
    hd_s, _ = rglru(proj, *rg, state_d_h.reshape(DEC_BATCH, 2, 1, D_RNN),
                    seq=DEC_SEQ, nb=DEC_BATCH, row0=N_CTX)
    mixed = jnp.concatenate([jnp.concatenate([hc_p, hd_p], axis=1),
                             jnp.concatenate([hc_s, hd_s], axis=1)], axis=0)
    x = out_residual(mixed, w_out, x, mod, g[1], tm=512)
    rw = jnp.zeros((D_MODEL, LANES), F32).at[:, :N_EXPERTS].set(router_w)
    rb = jnp.zeros((1, LANES), F32).at[0, :N_EXPERTS].set(router_b)
    x = swiglu_mixer(x, mod, g[2], g[3], moe_w1, moe_w3, moe_w2, rw, rb, tm=1024, tf=512)
    states = (cf.reshape(BATCH, 2, C_HEADS, C_DK, C_DV), nf.reshape(BATCH, 2, C_HEADS, C_DK),
              mf[..., 0].reshape(BATCH, 2, C_HEADS), hfin.reshape(BATCH, 2, D_RNN))
    return x, states


def kernel(x_prompt, x_sample, c, cache_a_k, cache_a_v, cache_b_k, cache_b_v, state_c_C, state_c_n, state_c_m, state_d_h, c_ctx, mod_w, mod_b, norm_g, w_in_even, w_out_even, a_lam, a_subln_g, b_rpb, ff_w1, ff_w3, ff_w2, w_in_odd, c_gate_b, conv_c, conv_d, conv_d_b, rg_wa, rg_ba, rg_wx, rg_bx, rg_lam, c_norm_g, w_out_odd, router_w, router_b, moe_w1, moe_w3, moe_w2):
    x = jnp.concatenate([x_prompt.reshape(N_CTX, D_MODEL), x_sample.reshape(N_LAT, D_MODEL)], axis=0)
    cvec = jnp.concatenate([c_ctx[None], c, jnp.zeros((MOD_ROWS - 1 - DEC_BATCH, D_MODEL), F32)], axis=0)
    mods = adaln_all(cvec, mod_w, mod_b)
    ca_k = cache_a_k.reshape(DEC_BATCH, N_EVEN, PAST_LEN, A_W)
    ca_v = cache_a_v.reshape(DEC_BATCH, N_EVEN, PAST_LEN, A_W)
    cb_k = cache_b_k.reshape(DEC_BATCH, N_EVEN, PAST_LEN, B_W)
    cb_v = cache_b_v.reshape(DEC_BATCH, N_EVEN, PAST_LEN, B_W)
    new_kv = []
    new_st = []
    for l in range(DEPTH):
        if l % 2 == 0:
            e = l // 2
            lam_init = 0.8 - 0.6 * math.exp(-0.3 * l)
            x, kv = _even_layer(x, mods[l], norm_g[l], e, lam_init, w_in_even[e], w_out_even[e], a_lam[e],
                                a_subln_g[e], b_rpb[e], ca_k, ca_v, cb_k, cb_v, ff_w1[e], ff_w3[e], ff_w2[e])
            new_kv.append(kv)
        else:
            j = l // 2
            x, st = _odd_layer(x, mods[l], norm_g[l], j, w_in_odd[j], c_gate_b[j], conv_c[j], conv_d[j],
                               conv_d_b[j], rg_wa[j], rg_ba[j], rg_wx[j], rg_bx[j], rg_lam[j], c_norm_g[j],
                               w_out_odd[j], state_c_C[:, j], state_c_n[:, j], state_c_m[:, j], state_d_h[:, j],
                               router_w[j], router_b[j], moe_w1[j], moe_w3[j], moe_w2[j])
            new_st.append(st)
    y_p = x[:N_CTX].reshape(BATCH, SEQ, D_MODEL)
    y_s = x[N_CTX:].reshape(DEC_BATCH, DEC_SEQ, D_MODEL)
    kv_out = tuple(jnp.stack([kv[i] for kv in new_kv], axis=1) for i in range(4))
    st_out = tuple(jnp.stack([st[i] for st in new_st], axis=1) for i in range(4))
    return (y_p, y_s) + kv_out + st_out
```

```python
import functools
import math

import numpy as np
import jax
import jax.numpy as jnp
from jax import lax
from jax.experimental import pallas as pl
from jax.experimental.pallas import tpu as pltpu

D_MODEL = 1024
BATCH = 32
SEQ = 256
DEPTH = 4
DEC_BATCH = 4
DEC_SEQ = 1024
PAST_LEN = 256
GRID_W = 64
GRID_ROWS = DEC_SEQ // GRID_W
N_EVEN = (DEPTH + 1) // 2
N_ODD = DEPTH // 2
A_HEADS = 4
A_HD = 64
B_HEADS = 8
B_HD = 64
NA_ROWS = 8
NA_COLS = 16
C_HEADS = 4
C_DK = 128
C_DV = 128
C_CHUNK = 64
D_RNN = 512
D_BLOCKS = 8
D_BW = D_RNN // D_BLOCKS
RG_C = 8.0
CONV_W = 4
FF_DENSE = 2816
N_EXPERTS = 8
FF_EXPERT = 2048
ROPE_BASE = 10000.0
EPS = 1e-6
NEG_INF = -1e30
A_W = A_HEADS * 2 * A_HD
B_W = B_HEADS * B_HD
C_W = C_HEADS * C_DK
IN_EVEN = 3 * A_W + 3 * B_W
IN_ODD_PAD = 4 * C_W + 2 * D_RNN + 128
F32 = jnp.float32
BF16 = jnp.bfloat16

N_CTX = BATCH * SEQ
N_LAT = DEC_BATCH * DEC_SEQ
N_TOK = N_CTX + N_LAT
MOD_ROWS = 8
LANES = 128
VMEM_LIMIT = 56 * 1024 * 1024
HI = lax.Precision.HIGHEST


def _cparams(n_axes):
    return pltpu.CompilerParams(dimension_semantics=("arbitrary",) * n_axes,
                                vmem_limit_bytes=VMEM_LIMIT)


def _sigmoid(x):
    return 1.0 / (1.0 + jnp.exp(-x))


def _silu(x):
    return x * _sigmoid(x)


def _softplus(x):
    return jnp.maximum(x, 0.0) + jnp.log1p(jnp.exp(-jnp.abs(x)))


def _gelu_tanh(x):
    return 0.5 * x * (1.0 + jnp.tanh(math.sqrt(2.0 / math.pi) * (x + 0.044715 * (x * x * x))))


def _rms(x, g):
    return x * lax.rsqrt(jnp.mean(x * x, axis=-1, keepdims=True) + EPS) * g


def _norm_mod(x, g, scale, shift):
    return _rms(x, g) * (1.0 + scale) + shift


def _dot(a, b):
    return jnp.dot(a, b, preferred_element_type=F32)


def _dot_nt(a, b):
    return lax.dot_general(a, b, (((1,), (1,)), ((), ())), preferred_element_type=F32)


def _dot_tn(a, b):
    return lax.dot_general(a, b, (((0,), (0,)), ((), ())), preferred_element_type=F32)


def _softmax_rows(s):
    e = jnp.exp(s - jnp.max(s, axis=-1, keepdims=True))
    return e / jnp.sum(e, axis=-1, keepdims=True)


def _mod_row(i, tm):
    r0 = i * tm
    return jnp.where(r0 < N_CTX, 0, 1 + (r0 - N_CTX) // DEC_SEQ)


def _mod_spec(which, tm):
    return pl.BlockSpec((1, 1, D_MODEL), lambda i, *_: (which * MOD_ROWS + _mod_row(i, tm), 0, 0))


def _adaln_kernel(c_ref, w_ref, b_ref, o_ref):
    s = _silu(c_ref[...]).astype(BF16)
    o_ref[0] = _dot(s, w_ref[0].astype(BF16)) + b_ref[0]


def adaln_all(cvec, mod_w, mod_b):
    tn = 1536
    out = pl.pallas_call(
        _adaln_kernel,
        grid=(DEPTH, 6 * D_MODEL // tn),
        in_specs=[pl.BlockSpec((MOD_ROWS, D_MODEL), lambda l, j: (0, 0)),
                  pl.BlockSpec((1, D_MODEL, tn), lambda l, j: (l, 0, j)),
                  pl.BlockSpec((1, 1, tn), lambda l, j: (l, 0, j))],
        out_specs=pl.BlockSpec((1, MOD_ROWS, tn), lambda l, j: (l, 0, j)),
        out_shape=jax.ShapeDtypeStruct((DEPTH, MOD_ROWS, 6 * D_MODEL), F32),
        compiler_params=_cparams(2),
        name="adaln",
    )(cvec, mod_w, mod_b.reshape(DEPTH, 1, 6 * D_MODEL))
    out = out.reshape(DEPTH, MOD_ROWS, 6, D_MODEL).transpose(0, 2, 1, 3)
    return out.reshape(DEPTH, 6 * MOD_ROWS, 1, D_MODEL)


def _nm_matmul_kernel(x_ref, sh_ref, sc_ref, g_ref, w_ref, o_ref, h_ref):
    @pl.when(pl.program_id(1) == 0)
    def _():
        h_ref[...] = _norm_mod(x_ref[...], g_ref[...], sc_ref[0], sh_ref[0]).astype(BF16)

    o_ref[...] = _dot(h_ref[...], w_ref[...].astype(BF16))


def nm_matmul(x, mod, g_pre, w, *, tm, tn):
    n = w.shape[1]
    return pl.pallas_call(
        _nm_matmul_kernel,
        grid=(N_TOK // tm, n // tn),
        in_specs=[pl.BlockSpec((tm, D_MODEL), lambda i, j: (i, 0)),
                  _mod_spec(0, tm), _mod_spec(1, tm),
                  pl.BlockSpec((1, D_MODEL), lambda i, j: (0, 0)),
                  pl.BlockSpec((D_MODEL, tn), lambda i, j: (0, j))],
        out_specs=pl.BlockSpec((tm, tn), lambda i, j: (i, j)),
        out_shape=jax.ShapeDtypeStruct((N_TOK, n), F32),
        scratch_shapes=[pltpu.VMEM((tm, D_MODEL), BF16)],
        compiler_params=_cparams(2),
        name="nm_matmul",
    )(x, mod, mod, g_pre.reshape(1, D_MODEL), w)


def _out_res_kernel(a_ref, w_ref, x_ref, gate_ref, g_ref, o_ref):
    out = _dot(a_ref[...].astype(BF16), w_ref[...].astype(BF16))
    o_ref[...] = x_ref[...] + gate_ref[0] * _rms(out, g_ref[...])


def out_residual(a, w, x, mod, g_post, *, tm):
    k = a.shape[1]
    return pl.pallas_call(
        _out_res_kernel,
        grid=(N_TOK // tm,),
        in_specs=[pl.BlockSpec((tm, k), lambda i: (i, 0)),
                  pl.BlockSpec((k, D_MODEL), lambda i: (0, 0)),
                  pl.BlockSpec((tm, D_MODEL), lambda i: (i, 0)),
                  _mod_spec(2, tm),
                  pl.BlockSpec((1, D_MODEL), lambda i: (0, 0))],
        out_specs=pl.BlockSpec((tm, D_MODEL), lambda i: (i, 0)),
        out_shape=jax.ShapeDtypeStruct((N_TOK, D_MODEL), F32),
        compiler_params=_cparams(1),
        name="out_residual",
    )(a, w, x, mod, g_post.reshape(1, D_MODEL))


def _lam_value(lam_ref, lam_init):
    lv = lam_ref[...]
    a = jnp.exp(jnp.sum(lv[0:1] * lv[1:2], axis=-1, keepdims=True))
    b = jnp.exp(jnp.sum(lv[2:3] * lv[3:4], axis=-1, keepdims=True))
    return a - b + lam_init


def _diff_attention(q, k, v, lam, subln_g, lam_init):
    outs = []
    for h in range(A_HEADS):
        ps = []
        for m in range(2):
            lo = (2 * h + m) * A_HD
            s = _dot_nt(q[:, lo:lo + A_HD], k[:, lo:lo + A_HD]) * (A_HD ** -0.5)
            ps.append(_softmax_rows(s))
        p = ps[0] - lam * ps[1]
        o = _dot(p.astype(BF16), v[:, 2 * A_HD * h:2 * A_HD * (h + 1)])
        outs.append(_rms(o, subln_g) * (1.0 - lam_init))
    return outs


def _attn_ctx_kernel(qa_ref, ka_ref, va_ref, qb_ref, kb_ref, vb_ref, lam_ref, sg_ref, o_ref, *, lam_init):
    lam = _lam_value(lam_ref, lam_init)
    outs = _diff_attention(qa_ref[...].astype(BF16), ka_ref[...].astype(BF16), va_ref[...].astype(BF16),
                           lam, sg_ref[...], lam_init)
    for h in range(A_HEADS):
        o_ref[:, 2 * A_HD * h:2 * A_HD * (h + 1)] = outs[h]
    qb = qb_ref[...].astype(BF16)
    kb = kb_ref[...].astype(BF16)
    vb = vb_ref[...].astype(BF16)
    for h in range(B_HEADS):
        sl = slice(B_HD * h, B_HD * (h + 1))
        p = _softmax_rows(_dot_nt(qb[:, sl], kb[:, sl]) * (B_HD ** -0.5))
        o_ref[:, A_W + B_HD * h:A_W + B_HD * (h + 1)] = _dot(p.astype(BF16), vb[:, sl])


def attn_ctx(proj, a_lam, subln_g, lam_init):
    specs = [pl.BlockSpec((SEQ, A_W), lambda b, c=c: (b, c)) for c in range(6)]
    return pl.pallas_call(
        functools.partial(_attn_ctx_kernel, lam_init=lam_init),
        grid=(BATCH,),
        in_specs=specs + [pl.BlockSpec((4, A_HD), lambda b: (0, 0)),
                          pl.BlockSpec((1, 2 * A_HD), lambda b: (0, 0))],
        out_specs=pl.BlockSpec((SEQ, A_W + B_W), lambda b: (b, 0)),
        out_shape=jax.ShapeDtypeStruct((N_CTX, A_W + B_W), F32),
        compiler_params=_cparams(1),
        name="attn_ctx",
    )(proj, proj, proj, proj, proj, proj, a_lam, subln_g.reshape(1, 2 * A_HD))


def _rope_tables():
    half = A_HD // 2
    t = np.arange(DEC_SEQ)
    pos = np.stack([t // GRID_W, t % GRID_W], axis=-1).astype(np.float32)
    freqs = (np.float32(ROPE_BASE) ** (-np.arange(0, half, 2, dtype=np.float32) / np.float32(half))).astype(np.float32)
    ang = (pos[:, :, None] * freqs).astype(np.float32)
    cos = np.cos(ang).astype(np.float32)
    sin = np.sin(ang).astype(np.float32)
    cos_h = np.concatenate([cos[:, 0], cos[:, 0], cos[:, 1], cos[:, 1]], axis=-1)
    sin_h = np.concatenate([-sin[:, 0], sin[:, 0], -sin[:, 1], sin[:, 1]], axis=-1)
    reps = A_W // A_HD
    return np.tile(cos_h, (1, reps)), np.tile(sin_h, (1, reps))


def _rope(x, cos, sin):
    quarter = A_HD // 4
    cols = []
    for c in range(x.shape[1] // LANES):
        xs = x[:, c * LANES:(c + 1) * LANES]
        up = pltpu.roll(xs, LANES - quarter, 1)
        dn = pltpu.roll(xs, quarter, 1)
        lane = lax.broadcasted_iota(jnp.int32, xs.shape, 1)
        cols.append(jnp.where((lane % (2 * quarter)) < quarter, up, dn))
    partner = jnp.concatenate(cols, axis=1)
    return x * cos + partner * sin


def _attn_lat_a_kernel(q_ref, k_ref, v_ref, ck_ref, cv_ref, cosq_ref, sinq_ref, cosk_ref, sink_ref,
                       lam_ref, sg_ref, o_ref, k_s, v_s, *, lam_init):
    @pl.when(pl.program_id(1) == 0)
    def _():
        k_s[0:DEC_SEQ, :] = _rope(k_ref[...], cosk_ref[...], sink_ref[...]).astype(BF16)
        k_s[DEC_SEQ:DEC_SEQ + PAST_LEN, :] = ck_ref[...].astype(BF16)
        v_s[0:DEC_SEQ, :] = v_ref[...].astype(BF16)
        v_s[DEC_SEQ:DEC_SEQ + PAST_LEN, :] = cv_ref[...].astype(BF16)

    lam = _lam_value(lam_ref, lam_init)
    q = _rope(q_ref[...], cosq_ref[...], sinq_ref[...]).astype(BF16)
    outs = _diff_attention(q, k_s[...], v_s[...], lam, sg_ref[...], lam_init)
    for h in range(A_HEADS):
        o_ref[:, 2 * A_HD * h:2 * A_HD * (h + 1)] = outs[h]


def attn_lat_a(proj, cache_k, cache_v, e, a_lam, subln_g, lam_init, *, tq):
    cos, sin = _rope_tables()
    nq = DEC_SEQ // tq
    q0 = N_CTX // tq
    b0 = N_CTX // DEC_SEQ
    cache_spec = pl.BlockSpec((None, None, PAST_LEN, A_W), lambda b, i: (b, e, 0, 0))
    return pl.pallas_call(
        functools.partial(_attn_lat_a_kernel, lam_init=lam_init),
        grid=(DEC_BATCH, nq),
        in_specs=[pl.BlockSpec((tq, A_W), lambda b, i: (q0 + b * nq + i, 0)),
                  pl.BlockSpec((DEC_SEQ, A_W), lambda b, i: (b0 + b, 1)),
                  pl.BlockSpec((DEC_SEQ, A_W), lambda b, i: (b0 + b, 2)),
                  cache_spec, cache_spec,
                  pl.BlockSpec((tq, A_W), lambda b, i: (i, 0)),
                  pl.BlockSpec((tq, A_W), lambda b, i: (i, 0)),
                  pl.BlockSpec((DEC_SEQ, A_W), lambda b, i: (0, 0)),
                  pl.BlockSpec((DEC_SEQ, A_W), lambda b, i: (0, 0)),
                  pl.BlockSpec((4, A_HD), lambda b, i: (0, 0)),
                  pl.BlockSpec((1, 2 * A_HD), lambda b, i: (0, 0))],
        out_specs=pl.BlockSpec((tq, A_W), lambda b, i: (b * nq + i, 0)),
        out_shape=jax.ShapeDtypeStruct((N_LAT, A_W), F32),
        scratch_shapes=[pltpu.VMEM((DEC_SEQ + PAST_LEN, A_W), BF16),
                        pltpu.VMEM((DEC_SEQ + PAST_LEN, A_W), BF16)],
        compiler_params=_cparams(2),
        name="attn_lat_a",
    )(proj, proj, proj, cache_k, cache_v, cos, sin, cos, sin, a_lam, subln_g.reshape(1, 2 * A_HD))


NA_WIN = NA_ROWS * GRID_W


def _na_bias_kernel(rpb_ref, rsel_ref, csel_ref, valid_ref, o_ref):
    rows = jnp.dot(rsel_ref[...], rpb_ref[0], precision=HI, preferred_element_type=F32)
    full = jnp.dot(rows, csel_ref[...], precision=HI, preferred_element_type=F32)
    o_ref[0] = jnp.where(valid_ref[...] > 0.5, full, NEG_INF)


def _na_bias(rpb):
    n_dr = 2 * NA_ROWS
    n_dc = 2 * NA_COLS
    r = np.arange(GRID_ROWS)
    r0 = np.clip(r - NA_ROWS // 2, 0, GRID_ROWS - NA_ROWS)
    krow = r0[:, None] + np.arange(NA_ROWS)
    dr = (krow - r[:, None] + NA_ROWS - 1).reshape(-1)
    qcol = np.arange(GRID_W)
    kcol = np.arange(GRID_W)
    cstart = np.clip(qcol - NA_COLS // 2, 0, GRID_W - NA_COLS)
    valid = (kcol[None, :] >= cstart[:, None]) & (kcol[None, :] < cstart[:, None] + NA_COLS)
    dc = np.clip(kcol[None, :] - qcol[:, None] + NA_COLS - 1, 0, 2 * NA_COLS - 2).reshape(-1)
    rsel = (dr[:, None] == np.arange(n_dr)[None, :]).astype(np.float32)
    csel = (np.arange(n_dc)[:, None] == dc[None, :]).astype(np.float32)
    rpb_p = jnp.pad(rpb.astype(F32), ((0, 0), (0, 1), (0, 1)))
    nrk = GRID_ROWS * NA_ROWS
    nqk = GRID_W * GRID_W
    full = lambda shape: pl.BlockSpec(shape, lambda h: (0,) * len(shape))
    bias = pl.pallas_call(
        _na_bias_kernel,
        grid=(B_HEADS,),
        in_specs=[pl.BlockSpec((1, n_dr, n_dc), lambda h: (h, 0, 0)),
                  full((nrk, n_dr)), full((n_dc, nqk)), full((1, nqk))],
        out_specs=pl.BlockSpec((1, nrk, nqk), lambda h: (h, 0, 0)),
        out_shape=jax.ShapeDtypeStruct((B_HEADS, nrk, nqk), F32),
        compiler_params=_cparams(1),
        name="na_bias",
    )(rpb_p, rsel, csel, valid.reshape(1, nqk).astype(np.float32))
    bias = bias.reshape(B_HEADS, GRID_ROWS, NA_ROWS, GRID_W, GRID_W).transpose(0, 1, 3, 2, 4)
    return bias.reshape(B_HEADS, GRID_ROWS, GRID_W, NA_WIN)


def _attn_lat_b_kernel(q_ref, k_ref, v_ref, ck_ref, cv_ref, bias_ref, o_ref):
    r = pl.program_id(1)
    r0 = jnp.clip(r - NA_ROWS // 2, 0, GRID_ROWS - NA_ROWS)
    win = pl.ds(pl.multiple_of(r0 * GRID_W, GRID_W), NA_WIN)
    q = q_ref[...].astype(BF16)
    kw = k_ref[win, :].astype(BF16)
    vw = v_ref[win, :].astype(BF16)
    kc = ck_ref[...].astype(BF16)
    vc = cv_ref[...].astype(BF16)
    scale = B_HD ** -0.5
    for h in range(B_HEADS):
        sl = slice(B_HD * h, B_HD * (h + 1))
        s_win = _dot_nt(q[:, sl], kw[:, sl]) * scale + bias_ref[h, 0]
        s_ctx = _dot_nt(q[:, sl], kc[:, sl]) * scale
        m = jnp.maximum(jnp.max(s_win, axis=-1, keepdims=True), jnp.max(s_ctx, axis=-1, keepdims=True))
        e_win = jnp.exp(s_win - m)
        e_ctx = jnp.exp(s_ctx - m)
        den = jnp.sum(e_win, axis=-1, keepdims=True) + jnp.sum(e_ctx, axis=-1, keepdims=True)
        p_win = (e_win / den).astype(BF16)
        p_ctx = (e_ctx / den).astype(BF16)
        o_ref[:, sl] = _dot(p_win, vw[:, sl]) + _dot(p_ctx, vc[:, sl])


def attn_lat_b(proj, cache_k, cache_v, e, rpb):
    bias = _na_bias(rpb)
    q0 = N_CTX // GRID_W
    b0 = N_CTX // DEC_SEQ
    cache_spec = pl.BlockSpec((None, None, PAST_LEN, B_W), lambda b, r: (b, e, 0, 0))
    return pl.pallas_call(
        _attn_lat_b_kernel,
        grid=(DEC_BATCH, GRID_ROWS),
        in_specs=[pl.BlockSpec((GRID_W, B_W), lambda b, r: (q0 + b * GRID_ROWS + r, 3)),
                  pl.BlockSpec((DEC_SEQ, B_W), lambda b, r: (b0 + b, 4)),
                  pl.BlockSpec((DEC_SEQ, B_W), lambda b, r: (b0 + b, 5)),
                  cache_spec, cache_spec,
                  pl.BlockSpec((B_HEADS, 1, GRID_W, NA_WIN), lambda b, r: (0, r, 0, 0))],
        out_specs=pl.BlockSpec((GRID_W, B_W), lambda b, r: (b * GRID_ROWS + r, 0)),
        out_shape=jax.ShapeDtypeStruct((N_LAT, B_W), F32),
        compiler_params=_cparams(2),
        name="attn_lat_b",
    )(proj, proj, proj, cache_k, cache_v, bias)


CONV_PAD = 8


def _conv_centred(x_ref, pad_ref, w_ref, seq):
    c = pad_ref.shape[1]
    pad_ref[0:CONV_PAD, :] = jnp.zeros((CONV_PAD, c), F32)
    pad_ref[CONV_PAD + seq:2 * CONV_PAD + seq, :] = jnp.zeros((CONV_PAD, c), F32)
    pad_ref[CONV_PAD:CONV_PAD + seq, :] = x_ref[...]
    left = (CONV_W - 1) // 2
    acc = None
    for j in range(CONV_W):
        start = CONV_PAD - left + j
        term = w_ref[j:j + 1, :] * pad_ref[start:start + seq, :]
        acc = term if acc is None else acc + term
    return acc


def _mlstm_kernel(qk_ref, v_ref, o_ref, g_ref, gt_ref, gb_ref, gbt_ref, cw_ref, ng_ref,
                  c0_ref, n0_ref, m0_ref, out_ref, cf_ref, nf_ref, mf_ref,
                  pad_s, qk_s, acc_s, c_s, n_s, m_s, *, seq):
    L = C_CHUNK
    nc = seq // L
    qk_s[...] = _silu(_conv_centred(qk_ref, pad_s, cw_ref, seq))
    row = lax.broadcasted_iota(jnp.int32, (L, L), 0)
    col = lax.broadcasted_iota(jnp.int32, (L, L), 1)
    gb = gb_ref[...]
    gbt = gbt_ref[...]
    ng = ng_ref[...]
    for d in range(2):
        tri = (col <= row) if d == 0 else (col >= row)
        tri_f = tri.astype(F32)
        tri_t = ((row <= col) if d == 0 else (row >= col)).astype(F32)
        last = L - 1 if d == 0 else 0
        for h in range(C_HEADS):
            c_s[h] = c0_ref[d * C_HEADS + h]
        n_s[...] = n0_ref[d * C_HEADS:(d + 1) * C_HEADS]
        m_s[...] = m0_ref[d * C_HEADS:(d + 1) * C_HEADS]

        def chunk_step(ci, carry, d=d, tri=tri, tri_f=tri_f, tri_t=tri_t, last=last):
            c_idx = ci if d == 0 else nc - 1 - ci
            rows = pl.ds(pl.multiple_of(c_idx * L, L), L)
            g = g_ref[rows, :] + gb
            gt = gt_ref[c_idx] + gbt
            ls = -_softplus(-g)
            lst = -_softplus(-gt)
            bcol = jnp.dot(tri_f, ls, precision=HI, preferred_element_type=F32)
            brow = jnp.dot(lst, tri_t, precision=HI, preferred_element_type=F32)
            for h in range(C_HEADS):
                ci_ = d * 2 * C_HEADS + h
                cf_ = ci_ + C_HEADS
                b_c = bcol[:, cf_:cf_ + 1]
                b_r = brow[cf_:cf_ + 1, :]
                i_c = g[:, ci_:ci_ + 1]
                i_r = gt[ci_:ci_ + 1, :]
                m_prev = m_s[h][:, 0:1]
                n_prev = n_s[h]
                c_prev = c_s[h]
                dlog = jnp.where(tri, b_c - b_r + i_r, -jnp.inf)
                inter = b_c + m_prev
                m_t = jnp.maximum(inter, jnp.max(dlog, axis=-1, keepdims=True))
                w_in = jnp.exp(dlog - m_t)
                w_st = jnp.exp(inter - m_t)
                q = qk_s[rows, C_DK * h:C_DK * (h + 1)] * (C_DK ** -0.5)
                k = qk_s[rows, C_W + C_DK * h:C_W + C_DK * (h + 1)]
                v = v_ref[rows, C_DV * h:C_DV * (h + 1)]
                qb = q.astype(BF16)
                kb = k.astype(BF16)
                vb = v.astype(BF16)
                s = _dot_nt(qb, kb) * w_in
                num = w_st * _dot(qb, c_prev.astype(BF16)) + _dot(s.astype(BF16), vb)
                nq = w_st * jnp.sum(q * n_prev, axis=-1, keepdims=True) + jnp.sum(s, axis=-1, keepdims=True)
                hh = num / jnp.maximum(jnp.abs(nq), jnp.exp(-m_t))
                b_last = b_c[last:last + 1, :]
                wlog = b_last - b_c + i_c
                m_new = jnp.maximum(b_last + m_prev, jnp.max(wlog, axis=0, keepdims=True))
                ws = jnp.exp(wlog - m_new)
                wc = jnp.exp(b_last + m_prev - m_new)
                kw = ws * k
                c_s[h] = wc * c_prev + _dot_tn(kw.astype(BF16), vb)
                n_s[h] = wc * n_prev + jnp.sum(kw, axis=0, keepdims=True)
                m_s[h] = jnp.broadcast_to(m_new, (1, LANES))
                hs = slice(C_DV * h, C_DV * (h + 1))
                if d == 0:
                    acc_s[rows, hs] = hh
                else:
                    tot = acc_s[rows, hs] + hh
                    out_ref[rows, hs] = _rms(tot, ng) * _sigmoid(o_ref[rows, hs])
            return carry

        lax.fori_loop(0, nc, chunk_step, 0)
        for h in range(C_HEADS):
            cf_ref[d * C_HEADS + h] = c_s[h]
        nf_ref[d * C_HEADS:(d + 1) * C_HEADS] = n_s[...]
        mf_ref[d * C_HEADS:(d + 1) * C_HEADS] = m_s[...]


def mlstm(proj, gates_t, gate_b, conv_c, norm_g, c0, n0, m0, *, seq, nb, row0):
    nh = 2 * C_HEADS
    b0 = row0 // seq
    nc = seq // C_CHUNK
    gate_cols = (IN_ODD_PAD - LANES) // LANES
    gb = jnp.zeros((1, LANES), F32).at[0, :4 * C_HEADS].set(gate_b)
    out_shapes = (jax.ShapeDtypeStruct((nb * seq, C_W), F32),
                  jax.ShapeDtypeStruct((nb, nh, C_DK, C_DV), F32),
                  jax.ShapeDtypeStruct((nb, nh, 1, C_DK), F32),
                  jax.ShapeDtypeStruct((nb, nh, 1, LANES), F32))
    state_specs = [pl.BlockSpec((None, nh, C_DK, C_DV), lambda b: (b, 0, 0, 0)),
                   pl.BlockSpec((None, nh, 1, C_DK), lambda b: (b, 0, 0, 0)),
                   pl.BlockSpec((None, nh, 1, LANES), lambda b: (b, 0, 0, 0))]
    return pl.pallas_call(
        functools.partial(_mlstm_kernel, seq=seq),
        grid=(nb,),
        in_specs=[pl.BlockSpec((seq, 2 * C_W), lambda b: (b0 + b, 0)),
                  pl.BlockSpec((seq, C_W), lambda b: (b0 + b, 2)),
                  pl.BlockSpec((seq, C_W), lambda b: (b0 + b, 3)),
                  pl.BlockSpec((seq, LANES), lambda b: (b0 + b, gate_cols)),
                  pl.BlockSpec((nc, 4 * C_HEADS, C_CHUNK), lambda b: (b, 0, 0)),
                  pl.BlockSpec((1, LANES), lambda b: (0, 0)),
                  pl.BlockSpec((4 * C_HEADS, 1), lambda b: (0, 0)),
                  pl.BlockSpec((CONV_W, 2 * C_W), lambda b: (0, 0)),
                  pl.BlockSpec((1, C_DV), lambda b: (0, 0))] + state_specs,
        out_specs=[pl.BlockSpec((seq, C_W), lambda b: (b, 0))] + state_specs,
        out_shape=out_shapes,
        scratch_shapes=[pltpu.VMEM((seq + 2 * CONV_PAD, 2 * C_W), F32),
                        pltpu.VMEM((seq, 2 * C_W), F32),
                        pltpu.VMEM((seq, C_W), F32),
                        pltpu.VMEM((C_HEADS, C_DK, C_DV), F32),
                        pltpu.VMEM((C_HEADS, 1, C_DK), F32),
                        pltpu.VMEM((C_HEADS, 1, LANES), F32)],
        compiler_params=_cparams(1),
        name="mlstm",
    )(proj, proj, proj, proj, gates_t, gb, gate_b.reshape(4 * C_HEADS, 1), conv_c,
      norm_g.reshape(1, C_DV), c0, n0, m0)


RG_BLOCK = 64


def _rglru_kernel(xd_ref, gd_ref, cw_ref, cb_ref, wa_ref, wx_ref, ba_ref, bx_ref, lam_ref, h0_ref,
                  out_ref, hf_ref, pad_s, xc_s, acc_s, *, seq):
    R = RG_BLOCK
    nblk = seq // R
    xc_s[...] = _conv_centred(xd_ref, pad_s, cw_ref, seq) + cb_ref[...]
    row = lax.broadcasted_iota(jnp.int32, (R, D_RNN), 0)
    for d in range(2):
        sp = _softplus(-lam_ref[d])
        wa = wa_ref[d]
        wx = wx_ref[d]
        ba = ba_ref[d]
        bx = bx_ref[d]

        def block_step(bi, carry, d=d, sp=sp, wa=wa, wx=wx, ba=ba, bx=bx):
            b_idx = bi if d == 0 else nblk - 1 - bi
            rows = pl.ds(pl.multiple_of(b_idx * R, R), R)
            x = xc_s[rows, :]
            xb = x.astype(BF16)
            r = _sigmoid(_dot(xb, wa) + ba)
            ig = _sigmoid(_dot(xb, wx) + bx)
            log_a = -RG_C * r * sp
            a = jnp.exp(log_a)
            u = jnp.sqrt(-jnp.tanh(log_a) * (a * a + 1.0)) * (ig * x)
            sh = 1
            while sh < R:
                if d == 0:
                    a_s = pltpu.roll(a, sh, 0)
                    u_s = pltpu.roll(u, sh, 0)
                    ok = row >= sh
                else:
                    a_s = pltpu.roll(a, R - sh, 0)
                    u_s = pltpu.roll(u, R - sh, 0)
                    ok = row < R - sh
                u = u + a * jnp.where(ok, u_s, 0.0)
                a = a * jnp.where(ok, a_s, 1.0)
                sh *= 2
            hs = u + a * carry
            if d == 0:
                acc_s[rows, :] = hs
                return hs[R - 1:R, :]
            out_ref[rows, :] = (acc_s[rows, :] + hs) * _gelu_tanh(gd_ref[rows, :])
            return hs[0:1, :]

        hf_ref[d] = lax.fori_loop(0, nblk, block_step, h0_ref[d])


def _block_diag(w):
    eye = jnp.eye(D_BLOCKS, dtype=w.dtype)
    return (w[:, :, None, :] * eye[:, None, :, None]).reshape(D_RNN, D_RNN)


def rglru(proj, conv_d, conv_d_b, rg_wa, rg_ba, rg_wx, rg_bx, rg_lam, h0, *, seq, nb, row0):
    b0 = row0 // seq
    wa = jnp.stack([_block_diag(rg_wa[0]), _block_diag(rg_wa[1])]).astype(BF16)
    wx = jnp.stack([_block_diag(rg_wx[0]), _block_diag(rg_wx[1])]).astype(BF16)
    vec = lambda a: a.reshape(2, 1, D_RNN)
    full = lambda shape: pl.BlockSpec(shape, lambda b: (0,) * len(shape))
    return pl.pallas_call(
        functools.partial(_rglru_kernel, seq=seq),
        grid=(nb,),
        in_specs=[pl.BlockSpec((seq, D_RNN), lambda b: (b0 + b, 4)),
                  pl.BlockSpec((seq, D_RNN), lambda b: (b0 + b, 5)),
                  full((CONV_W, D_RNN)), full((1, D_RNN)),
                  full((2, D_RNN, D_RNN)), full((2, D_RNN, D_RNN)),
                  full((2, 1, D_RNN)), full((2, 1, D_RNN)), full((2, 1, D_RNN)),
                  pl.BlockSpec((None, 2, 1, D_RNN), lambda b: (b, 0, 0, 0))],
        out_specs=[pl.BlockSpec((seq, D_RNN), lambda b: (b, 0)),
                   pl.BlockSpec((None, 2, 1, D_RNN), lambda b: (b, 0, 0, 0))],
        out_shape=(jax.ShapeDtypeStruct((nb * seq, D_RNN), F32),
                   jax.ShapeDtypeStruct((nb, 2, 1, D_RNN), F32)),
        scratch_shapes=[pltpu.VMEM((seq + 2 * CONV_PAD, D_RNN), F32),
                        pltpu.VMEM((seq, D_RNN), F32),
                        pltpu.VMEM((seq, D_RNN), F32)],
        compiler_params=_cparams(1),
        name="rglru",
    )(proj, proj, conv_d, conv_d_b.reshape(1, D_RNN), wa, wx, vec(rg_ba), vec(rg_bx), vec(rg_lam), h0)


def _router_gates(h, rw_ref, rb_ref):
    logits = jnp.dot(h, rw_ref[...], precision=HI, preferred_element_type=F32) + rb_ref[...]
    lane = lax.broadcasted_iota(jnp.int32, logits.shape, 1)
    logits = jnp.where(lane < N_EXPERTS, logits, -jnp.inf)
    m1 = jnp.max(logits, axis=-1, keepdims=True)
    i1 = jnp.min(jnp.where(logits == m1, lane, LANES), axis=-1, keepdims=True)
    rest = jnp.where(lane == i1, -jnp.inf, logits)
    m2 = jnp.max(rest, axis=-1, keepdims=True)
    i2 = jnp.min(jnp.where(rest == m2, lane, LANES), axis=-1, keepdims=True)
    e2 = jnp.exp(m2 - m1)
    den = 1.0 + e2
    return jnp.where(lane == i1, 1.0 / den, 0.0) + jnp.where(lane == i2, e2 / den, 0.0)


def _router_top2(h, rw_ref, rb_ref):
    logits = jnp.dot(h, rw_ref[...], precision=HI, preferred_element_type=F32) + rb_ref[...]
    lane = lax.broadcasted_iota(jnp.int32, logits.shape, 1)
    logits = jnp.where(lane < N_EXPERTS, logits, -jnp.inf)
    m1 = jnp.max(logits, axis=-1, keepdims=True)
    i1 = jnp.min(jnp.where(logits == m1, lane, LANES), axis=-1, keepdims=True)
    rest = jnp.where(lane == i1, -jnp.inf, logits)
    m2 = jnp.max(rest, axis=-1, keepdims=True)
    i2 = jnp.min(jnp.where(rest == m2, lane, LANES), axis=-1, keepdims=True)
    e2 = jnp.exp(m2 - m1)
    den = 1.0 + e2
    gates = jnp.where(lane == i1, 1.0 / den, 0.0) + jnp.where(lane == i2, e2 / den, 0.0)
    chosen = jnp.where((lane == i1) | (lane == i2), 1.0, 0.0)
    return gates, chosen


MOE_MAIN = 384
MOE_EXTRA = 128
ROUTED_MARK = 2048.0


def _moe_kernel(x_ref, sh_ref, sc_ref, gate_ref, gpre_ref, gpost_ref, rw_ref, rb_ref,
                w1_ref, w3_ref, w2_ref, o_ref, h_s, acc_s, dg_s, vcol_s, vrow_s, cnt_s, xs_s, y_s, *, tm):
    e = pl.program_id(1)
    f = pl.program_id(2)
    last_f = pl.num_programs(2) - 1

    @pl.when((e == 0) & (f == 0))
    def _():
        h = _norm_mod(x_ref[...], gpre_ref[...], sc_ref[0], sh_ref[0])
        h_s[...] = h.astype(BF16)
        acc_s[...] = jnp.zeros_like(acc_s)
        gates, chosen = _router_top2(h, rw_ref, rb_ref)
        dg_s[...] = gates
        cnt_s[...] = jnp.sum(chosen, axis=0, keepdims=True)
        r = lax.broadcasted_iota(jnp.int32, (tm, tm), 0)
        c = lax.broadcasted_iota(jnp.int32, (tm, tm), 1)
        low = jnp.where(c < r, 1.0, jnp.where(c == r, ROUTED_MARK, 0.0)).astype(BF16)
        upp = jnp.where(r < c, 1.0, jnp.where(c == r, ROUTED_MARK, 0.0)).astype(BF16)
        ch = chosen.astype(BF16)
        vcol_s[...] = _dot(low, ch)
        vrow_s[...] = _dot_tn(ch, upp)

    lane = lax.broadcasted_iota(jnp.int32, (tm, LANES), 1)
    lane1 = lax.broadcasted_iota(jnp.int32, (1, LANES), 1)
    cnt_e = jnp.sum(jnp.where(lane1 == e, cnt_s[...], 0.0))

    def gather_rows(r0, nr):
        row = vrow_s[pl.ds(e, 1), :]
        want = lax.broadcasted_iota(jnp.int32, (nr, tm), 0).astype(F32) + (r0 + ROUTED_MARK)
        p = jnp.where(want == row, 1.0, 0.0).astype(BF16)
        xs_s[r0:r0 + nr, :] = _dot(p, h_s[...]).astype(BF16)
        y_s[r0:r0 + nr, :] = jnp.zeros((nr, D_MODEL), F32)

    def expert_rows(r0, nr):
        xs = xs_s[r0:r0 + nr, :]
        u = _silu(_dot(xs, w1_ref[0])) * _dot(xs, w3_ref[0])
        y_s[r0:r0 + nr, :] += _dot(u.astype(BF16), w2_ref[0])

    def scatter_rows(r0, nr):
        col = jnp.sum(jnp.where(lane == e, vcol_s[...], 0.0), axis=-1, keepdims=True)
        g = jnp.sum(jnp.where(lane == e, dg_s[...], 0.0), axis=-1, keepdims=True)
        want = lax.broadcasted_iota(jnp.int32, (tm, nr), 1).astype(F32) + (r0 + ROUTED_MARK)
        q = jnp.where(want == col, 1.0, 0.0).astype(BF16)
        acc_s[...] += g * _dot(q, y_s[r0:r0 + nr, :].astype(BF16))

    blocks = [(0, MOE_MAIN, None)]
    r0 = MOE_MAIN
    while r0 < tm:
        blocks.append((r0, MOE_EXTRA, cnt_e > r0))
        r0 += MOE_EXTRA
    for r0, nr, needed in blocks:
        first = (f == 0) if needed is None else ((f == 0) & needed)
        final = (f == last_f) if needed is None else ((f == last_f) & needed)
        pl.when(first)(functools.partial(gather_rows, r0, nr))
        if needed is None:
            expert_rows(r0, nr)
        else:
            pl.when(needed)(functools.partial(expert_rows, r0, nr))
        pl.when(final)(functools.partial(scatter_rows, r0, nr))

    @pl.when((e == pl.num_programs(1) - 1) & (f == last_f))
    def _():
        o_ref[...] = x_ref[...] + gate_ref[0] * _rms(acc_s[...], gpost_ref[...])


def moe_mixer(x, mod, g_pre, g_post, w1, w3, w2, router_w, router_b, *, tm, tf):
    n_e, _, ff = w1.shape
    n_rows = MOE_MAIN + MOE_EXTRA * (-(-(tm - MOE_MAIN) // MOE_EXTRA))
    const = lambda shape: pl.BlockSpec(shape, lambda i, e, f: (0,) * len(shape))
    return pl.pallas_call(
        functools.partial(_moe_kernel, tm=tm),
        grid=(N_TOK // tm, n_e, ff // tf),
        in_specs=[pl.BlockSpec((tm, D_MODEL), lambda i, e, f: (i, 0)),
                  _mod_spec(3, tm), _mod_spec(4, tm), _mod_spec(5, tm),
                  const((1, D_MODEL)), const((1, D_MODEL)),
                  const((D_MODEL, LANES)), const((1, LANES)),
                  pl.BlockSpec((1, D_MODEL, tf), lambda i, e, f: (e, 0, f)),
                  pl.BlockSpec((1, D_MODEL, tf), lambda i, e, f: (e, 0, f)),
                  pl.BlockSpec((1, tf, D_MODEL), lambda i, e, f: (e, f, 0))],
        out_specs=pl.BlockSpec((tm, D_MODEL), lambda i, e, f: (i, 0)),
        out_shape=jax.ShapeDtypeStruct((N_TOK, D_MODEL), F32),
        scratch_shapes=[pltpu.VMEM((tm, D_MODEL), BF16),
                        pltpu.VMEM((tm, D_MODEL), F32),
                        pltpu.VMEM((tm, LANES), F32),
                        pltpu.VMEM((tm, LANES), F32),
                        pltpu.VMEM((LANES, tm), F32),
                        pltpu.VMEM((1, LANES), F32),
                        pltpu.VMEM((n_rows, D_MODEL), BF16),
                        pltpu.VMEM((n_rows, D_MODEL), F32)],
        compiler_params=_cparams(3),
        name="moe_routed",
    )(x, mod, mod, mod, g_pre.reshape(1, D_MODEL), g_post.reshape(1, D_MODEL), router_w, router_b, w1, w3, w2)


def _swiglu_kernel(x_ref, sh_ref, sc_ref, gate_ref, gpre_ref, gpost_ref, rw_ref, rb_ref,
                   w1_ref, w3_ref, w2_ref, o_ref, h_s, acc_s, dg_s, *, routed):
    e = pl.program_id(1)
    f = pl.program_id(2)

    @pl.when((e == 0) & (f == 0))
    def _():
        h = _norm_mod(x_ref[...], gpre_ref[...], sc_ref[0], sh_ref[0])
        h_s[...] = h.astype(BF16)
        acc_s[...] = jnp.zeros_like(acc_s)
        if routed:
            dg_s[...] = _router_gates(h, rw_ref, rb_ref)

    h = h_s[...]
    u = _silu(_dot(h, w1_ref[0].astype(BF16))) * _dot(h, w3_ref[0].astype(BF16))
    if routed:
        lane = lax.broadcasted_iota(jnp.int32, dg_s.shape, 1)
        u = u * jnp.sum(jnp.where(lane == e, dg_s[...], 0.0), axis=-1, keepdims=True)
    acc_s[...] += _dot(u.astype(BF16), w2_ref[0].astype(BF16))

    @pl.when((e == pl.num_programs(1) - 1) & (f == pl.num_programs(2) - 1))
    def _():
        o_ref[...] = x_ref[...] + gate_ref[0] * _rms(acc_s[...], gpost_ref[...])


def swiglu_mixer(x, mod, g_pre, g_post, w1, w3, w2, router_w=None, router_b=None, *, tm, tf):
    routed = router_w is not None
    n_e, _, ff = w1.shape
    if not routed:
        router_w = jnp.zeros((D_MODEL, LANES), F32)
        router_b = jnp.zeros((1, LANES), F32)
    const = lambda shape: pl.BlockSpec(shape, lambda i, e, f: (0,) * len(shape))
    return pl.pallas_call(
        functools.partial(_swiglu_kernel, routed=routed),
        grid=(N_TOK // tm, n_e, ff // tf),
        in_specs=[pl.BlockSpec((tm, D_MODEL), lambda i, e, f: (i, 0)),
                  _mod_spec(3, tm), _mod_spec(4, tm), _mod_spec(5, tm),
                  const((1, D_MODEL)), const((1, D_MODEL)),
                  const((D_MODEL, LANES)), const((1, LANES)),
                  pl.BlockSpec((1, D_MODEL, tf), lambda i, e, f: (e, 0, f)),
                  pl.BlockSpec((1, D_MODEL, tf), lambda i, e, f: (e, 0, f)),
                  pl.BlockSpec((1, tf, D_MODEL), lambda i, e, f: (e, f, 0))],
        out_specs=pl.BlockSpec((tm, D_MODEL), lambda i, e, f: (i, 0)),
        out_shape=jax.ShapeDtypeStruct((N_TOK, D_MODEL), F32),
        scratch_shapes=[pltpu.VMEM((tm, D_MODEL), BF16),
                        pltpu.VMEM((tm, D_MODEL), F32),
                        pltpu.VMEM((tm, LANES), F32)],
        compiler_params=_cparams(3),
        name="swiglu_routed" if routed else "swiglu_dense",
    )(x, mod, mod, mod, g_pre.reshape(1, D_MODEL), g_post.reshape(1, D_MODEL), router_w, router_b, w1, w3, w2)


def _even_layer(x, mod, g, e, lam_init, w_in, w_out, a_lam, subln_g, rpb, cache_a_k, cache_a_v,
                cache_b_k, cache_b_v, ff_w1, ff_w3, ff_w2):
    proj = nm_matmul(x, mod, g[0], w_in, tm=1024, tn=768)
    o_ctx = attn_ctx(proj, a_lam, subln_g, lam_init)
    o_a = attn_lat_a(proj, cache_a_k, cache_a_v, e, a_lam, subln_g, lam_init, tq=256)
    o_b = attn_lat_b(proj, cache_b_k, cache_b_v, e, rpb)
    mixed = jnp.concatenate([o_ctx, jnp.concatenate([o_a, o_b], axis=1)], axis=0)
    x = out_residual(mixed, w_out, x, mod, g[1], tm=512)
    x = swiglu_mixer(x, mod, g[2], g[3], ff_w1[None], ff_w3[None], ff_w2[None], tm=1024, tf=256)
    cols = lambda c0, c1: lax.slice(proj, (0, c0), (N_CTX, c1))
    new_kv = (cols(A_W, 2 * A_W).reshape(BATCH, SEQ, A_HEADS, 2 * A_HD),
              cols(2 * A_W, 3 * A_W).reshape(BATCH, SEQ, A_HEADS, 2 * A_HD),
              cols(3 * A_W + B_W, 3 * A_W + 2 * B_W).reshape(BATCH, SEQ, B_HEADS, B_HD),
              cols(3 * A_W + 2 * B_W, IN_EVEN).reshape(BATCH, SEQ, B_HEADS, B_HD))
    return x, new_kv


def _odd_w_in(w):
    g0 = 4 * C_W
    g1 = g0 + 4 * C_HEADS
    pad = jnp.zeros((D_MODEL, LANES - 4 * C_HEADS), w.dtype)
    return jnp.concatenate([w[:, :g0], w[:, g1:], w[:, g0:g1], pad], axis=1)


def _odd_layer(x, mod, g, j, w_in, gate_b, conv_c, conv_d, conv_d_b, rg_wa, rg_ba, rg_wx, rg_bx, rg_lam,
               c_norm_g, w_out, state_c_C, state_c_n, state_c_m, state_d_h,
               router_w, router_b, moe_w1, moe_w3, moe_w2):
    proj = nm_matmul(x, mod, g[0], _odd_w_in(w_in), tm=1024, tn=640)
    gates = proj[:, IN_ODD_PAD - LANES:IN_ODD_PAD - LANES + 4 * C_HEADS]
    gates_t = gates.reshape(N_TOK // C_CHUNK, C_CHUNK, 4 * C_HEADS).transpose(0, 2, 1)
    nh = 2 * C_HEADS
    nc_ctx = N_CTX // C_CHUNK
    zc = (jnp.zeros((BATCH, nh, C_DK, C_DV), F32), jnp.zeros((BATCH, nh, 1, C_DK), F32),
          jnp.zeros((BATCH, nh, 1, LANES), F32))
    sc = (state_c_C.reshape(DEC_BATCH, nh, C_DK, C_DV), state_c_n.reshape(DEC_BATCH, nh, 1, C_DK),
          jnp.broadcast_to(state_c_m.reshape(DEC_BATCH, nh, 1, 1), (DEC_BATCH, nh, 1, LANES)))
    hc_p, cf, nf, mf = mlstm(proj, gates_t[:nc_ctx], gate_b, conv_c, c_norm_g, *zc,
                             seq=SEQ, nb=BATCH, row0=0)
    hc_s, _, _, _ = mlstm(proj, gates_t[nc_ctx:], gate_b, conv_c, c_norm_g, *sc,
                          seq=DEC_SEQ, nb=DEC_BATCH, row0=N_CTX)
    rg = (conv_d, conv_d_b, rg_wa, rg_ba, rg_wx, rg_bx, rg_lam)
    hd_p, hfin = rglru(proj, *rg, jnp.zeros((BATCH, 2, 1, D_RNN), F32), seq=SEQ, nb=BATCH, row0=0)
    hd_s, _ = rglru(proj, *rg, state_d_h.reshape(DEC_BATCH, 2, 1, D_RNN),
                    seq=DEC_SEQ, nb=DEC_BATCH, row0=N_CTX)
    mixed = jnp.concatenate([jnp.concatenate([hc_p, hd_p], axis=1),
                             jnp.concatenate([hc_s, hd_s], axis=1)], axis=0)
    x = out_residual(mixed, w_out, x, mod, g[1], tm=512)
    rw = jnp.zeros((D_MODEL, LANES), F32).at[:, :N_EXPERTS].set(router_w)
    rb = jnp.zeros((1, LANES), F32).at[0, :N_EXPERTS].set(router_b)
    x = moe_mixer(x, mod, g[2], g[3], moe_w1.astype(BF16), moe_w3.astype(BF16), moe_w2.astype(BF16),
                  rw, rb, tm=1024, tf=1024)
    states = (cf.reshape(BATCH, 2, C_HEADS, C_DK, C_DV), nf.reshape(BATCH, 2, C_HEADS, C_DK),
              mf[..., 0].reshape(BATCH, 2, C_HEADS), hfin.reshape(BATCH, 2, D_RNN))
    return x, states


def kernel(x_prompt, x_sample, c, cache_a_k, cache_a_v, cache_b_k, cache_b_v, state_c_C, state_c_n, state_c_m, state_d_h, c_ctx, mod_w, mod_b, norm_g, w_in_even, w_out_even, a_lam, a_subln_g, b_rpb, ff_w1, ff_w3, ff_w2, w_in_odd, c_gate_b, conv_c, conv_d, conv_d_b, rg_wa, rg_ba, rg_wx, rg_bx, rg_lam, c_norm_g, w_out_odd, router_w, router_b, moe_w1, moe_w3, moe_w2):
    x = jnp.concatenate([x_prompt.reshape(N_CTX, D_MODEL), x_sample.reshape(N_LAT, D_MODEL)], axis=0)
    cvec = jnp.concatenate([c_ctx[None], c, jnp.zeros((MOD_ROWS - 1 - DEC_BATCH, D_MODEL), F32)], axis=0)
    mods = adaln_all(cvec, mod_w, mod_b)
    ca_k = cache_a_k.reshape(DEC_BATCH, N_EVEN, PAST_LEN, A_W)
    ca_v = cache_a_v.reshape(DEC_BATCH, N_EVEN, PAST_LEN, A_W)
    cb_k = cache_b_k.reshape(DEC_BATCH, N_EVEN, PAST_LEN, B_W)
    cb_v = cache_b_v.reshape(DEC_BATCH, N_EVEN, PAST_LEN, B_W)
    new_kv = []
    new_st = []
    for l in range(DEPTH):
        if l % 2 == 0:
            e = l // 2
            lam_init = 0.8 - 0.6 * math.exp(-0.3 * l)
            x, kv = _even_layer(x, mods[l], norm_g[l], e, lam_init, w_in_even[e], w_out_even[e], a_lam[e],
                                a_subln_g[e], b_rpb[e], ca_k, ca_v, cb_k, cb_v, ff_w1[e], ff_w3[e], ff_w2[e])
            new_kv.append(kv)
        else:
            j = l // 2
            x, st = _odd_layer(x, mods[l], norm_g[l], j, w_in_odd[j], c_gate_b[j], conv_c[j], conv_d[j],
                               conv_d_b[j], rg_wa[j], rg_ba[j], rg_wx[j], rg_bx[j], rg_lam[j], c_norm_g[j],
                               w_out_odd[j], state_c_C[:, j], state_c_n[:, j], state_c_m[:, j], state_d_h[:, j],
                               router_w[j], router_b[j], moe_w1[j], moe_w3[j], moe_w2[j])
            new_st.append(st)
    y_p = x[:N_CTX].reshape(BATCH, SEQ, D_MODEL)
    y_s = x[N_CTX:].reshape(DEC_BATCH, DEC_SEQ, D_MODEL)
    kv_out = tuple(jnp.stack([kv[i] for kv in new_kv], axis=1) for i in range(4))
    st_out = tuple(jnp.stack([st[i] for st in new_st], axis=1) for i in range(4))
    return (y_p, y_s) + kv_out + st_out
```

```python
import functools
import math

import numpy as np
import jax
import jax.numpy as jnp
from jax import lax
from jax.experimental import pallas as pl
from jax.experimental.pallas import tpu as pltpu

D_MODEL = 1024
BATCH = 32
SEQ = 256
DEPTH = 4
DEC_BATCH = 4
DEC_SEQ = 1024
PAST_LEN = 256
GRID_W = 64
GRID_ROWS = DEC_SEQ // GRID_W
N_EVEN = (DEPTH + 1) // 2
N_ODD = DEPTH // 2
A_HEADS = 4
A_HD = 64
B_HEADS = 8
B_HD = 64
NA_ROWS = 8
NA_COLS = 16
C_HEADS = 4
C_DK = 128
C_DV = 128
C_CHUNK = 64
D_RNN = 512
D_BLOCKS = 8
D_BW = D_RNN // D_BLOCKS
RG_C = 8.0
CONV_W = 4
FF_DENSE = 2816
N_EXPERTS = 8
FF_EXPERT = 2048
ROPE_BASE = 10000.0
EPS = 1e-6
NEG_INF = -1e30
A_W = A_HEADS * 2 * A_HD
B_W = B_HEADS * B_HD
C_W = C_HEADS * C_DK
IN_EVEN = 3 * A_W + 3 * B_W
IN_ODD_PAD = 4 * C_W + 2 * D_RNN + 128
F32 = jnp.float32
BF16 = jnp.bfloat16

N_CTX = BATCH * SEQ
N_LAT = DEC_BATCH * DEC_SEQ
N_TOK = N_CTX + N_LAT
MOD_ROWS = 8
LANES = 128
VMEM_LIMIT = 56 * 1024 * 1024
HI = lax.Precision.HIGHEST


def _cparams(n_axes):
    return pltpu.CompilerParams(dimension_semantics=("arbitrary",) * n_axes,
                                vmem_limit_bytes=VMEM_LIMIT)


def _sigmoid(x):
    return 1.0 / (1.0 + jnp.exp(-x))


def _silu(x):
    return x * _sigmoid(x)


def _softplus(x):
    return jnp.maximum(x, 0.0) + jnp.log1p(jnp.exp(-jnp.abs(x)))


def _gelu_tanh(x):
    return 0.5 * x * (1.0 + jnp.tanh(math.sqrt(2.0 / math.pi) * (x + 0.044715 * (x * x * x))))


def _rms(x, g):
    return x * lax.rsqrt(jnp.mean(x * x, axis=-1, keepdims=True) + EPS) * g


def _norm_mod(x, g, scale, shift):
    return _rms(x, g) * (1.0 + scale) + shift


def _dot(a, b):
    return jnp.dot(a, b, preferred_element_type=F32)


def _dot_nt(a, b):
    return lax.dot_general(a, b, (((1,), (1,)), ((), ())), preferred_element_type=F32)


def _dot_tn(a, b):
    return lax.dot_general(a, b, (((0,), (0,)), ((), ())), preferred_element_type=F32)


def _softmax_rows(s):
    e = jnp.exp(s - jnp.max(s, axis=-1, keepdims=True))
    return e / jnp.sum(e, axis=-1, keepdims=True)


def _mod_row(i, tm):
    r0 = i * tm
    return jnp.where(r0 < N_CTX, 0, 1 + (r0 - N_CTX) // DEC_SEQ)


def _mod_spec(which, tm):
    return pl.BlockSpec((1, 1, D_MODEL), lambda i, *_: (which * MOD_ROWS + _mod_row(i, tm), 0, 0))


def _adaln_kernel(c_ref, w_ref, b_ref, o_ref):
    s = _silu(c_ref[...]).astype(BF16)
    o_ref[0] = _dot(s, w_ref[0].astype(BF16)) + b_ref[0]


def adaln_all(cvec, mod_w, mod_b):
    tn = 1536
    out = pl.pallas_call(
        _adaln_kernel,
        grid=(DEPTH, 6 * D_MODEL // tn),
        in_specs=[pl.BlockSpec((MOD_ROWS, D_MODEL), lambda l, j: (0, 0)),
                  pl.BlockSpec((1, D_MODEL, tn), lambda l, j: (l, 0, j)),
                  pl.BlockSpec((1, 1, tn), lambda l, j: (l, 0, j))],
        out_specs=pl.BlockSpec((1, MOD_ROWS, tn), lambda l, j: (l, 0, j)),
        out_shape=jax.ShapeDtypeStruct((DEPTH, MOD_ROWS, 6 * D_MODEL), F32),
        compiler_params=_cparams(2),
        name="adaln",
    )(cvec, mod_w, mod_b.reshape(DEPTH, 1, 6 * D_MODEL))
    out = out.reshape(DEPTH, MOD_ROWS, 6, D_MODEL).transpose(0, 2, 1, 3)
    return out.reshape(DEPTH, 6 * MOD_ROWS, 1, D_MODEL)


def _nm_matmul_kernel(x_ref, sh_ref, sc_ref, g_ref, w_ref, o_ref, h_ref):
    @pl.when(pl.program_id(1) == 0)
    def _():
        h_ref[...] = _norm_mod(x_ref[...], g_ref[...], sc_ref[0], sh_ref[0]).astype(BF16)

    o_ref[...] = _dot(h_ref[...], w_ref[...].astype(BF16))


def nm_matmul(x, mod, g_pre, w, *, tm, tn):
    n = w.shape[1]
    return pl.pallas_call(
        _nm_matmul_kernel,
        grid=(N_TOK // tm, n // tn),
        in_specs=[pl.BlockSpec((tm, D_MODEL), lambda i, j: (i, 0)),
                  _mod_spec(0, tm), _mod_spec(1, tm),
                  pl.BlockSpec((1, D_MODEL), lambda i, j: (0, 0)),
                  pl.BlockSpec((D_MODEL, tn), lambda i, j: (0, j))],
        out_specs=pl.BlockSpec((tm, tn), lambda i, j: (i, j)),
        out_shape=jax.ShapeDtypeStruct((N_TOK, n), F32),
        scratch_shapes=[pltpu.VMEM((tm, D_MODEL), BF16)],
        compiler_params=_cparams(2),
        name="nm_matmul",
    )(x, mod, mod, g_pre.reshape(1, D_MODEL), w)


def _out_res_kernel(c1_ref, c2_ref, l1_ref, l2_ref, w_ref, x_ref, gate_ref, g_ref, o_ref, *, n_ctx_tiles):
    i = pl.program_id(0)
    half = w_ref.shape[0] // 2

    def finish(a1_ref, a2_ref):
        out = (_dot(a1_ref[...].astype(BF16), w_ref[0:half, :].astype(BF16))
               + _dot(a2_ref[...].astype(BF16), w_ref[half:, :].astype(BF16)))
        o_ref[...] = x_ref[...] + gate_ref[0] * _rms(out, g_ref[...])

    pl.when(i < n_ctx_tiles)(functools.partial(finish, c1_ref, c2_ref))
    pl.when(i >= n_ctx_tiles)(functools.partial(finish, l1_ref, l2_ref))


def out_residual(ctx1, ctx2, lat1, lat2, w, x, mod, g_post, *, tm):
    k = w.shape[0]
    nct = N_CTX // tm
    ctx_spec = lambda cb: pl.BlockSpec((tm, k // 2), lambda i: (jnp.minimum(i, nct - 1), cb))
    lat_spec = lambda cb: pl.BlockSpec((tm, k // 2), lambda i: (jnp.maximum(i - nct, 0), cb))
    return pl.pallas_call(
        functools.partial(_out_res_kernel, n_ctx_tiles=nct),
        grid=(N_TOK // tm,),
        in_specs=[ctx_spec(ctx1[1]), ctx_spec(ctx2[1]), lat_spec(lat1[1]), lat_spec(lat2[1]),
                  pl.BlockSpec((k, D_MODEL), lambda i: (0, 0)),
                  pl.BlockSpec((tm, D_MODEL), lambda i: (i, 0)),
                  _mod_spec(2, tm),
                  pl.BlockSpec((1, D_MODEL), lambda i: (0, 0))],
        out_specs=pl.BlockSpec((tm, D_MODEL), lambda i: (i, 0)),
        out_shape=jax.ShapeDtypeStruct((N_TOK, D_MODEL), F32),
        compiler_params=_cparams(1),
        name="out_residual",
    )(ctx1[0], ctx2[0], lat1[0], lat2[0], w, x, mod, g_post.reshape(1, D_MODEL))


def _lam_value(lam_ref, lam_init):
    lv = lam_ref[...]
    a = jnp.exp(jnp.sum(lv[0:1] * lv[1:2], axis=-1, keepdims=True))
    b = jnp.exp(jnp.sum(lv[2:3] * lv[3:4], axis=-1, keepdims=True))
    return a - b + lam_init


def _diff_attention(q, k, v, lam, subln_g, lam_init):
    outs = []
    for h in range(A_HEADS):
        ps = []
        for m in range(2):
            lo = (2 * h + m) * A_HD
            s = _dot_nt(q[:, lo:lo + A_HD], k[:, lo:lo + A_HD]) * (A_HD ** -0.5)
            ps.append(_softmax_rows(s))
        p = ps[0] - lam * ps[1]
        o = _dot(p.astype(BF16), v[:, 2 * A_HD * h:2 * A_HD * (h + 1)])
        outs.append(_rms(o, subln_g) * (1.0 - lam_init))
    return outs


def _attn_ctx_kernel(qa_ref, ka_ref, va_ref, qb_ref, kb_ref, vb_ref, lam_ref, sg_ref, o_ref, *, lam_init):
    lam = _lam_value(lam_ref, lam_init)
    outs = _diff_attention(qa_ref[...].astype(BF16), ka_ref[...].astype(BF16), va_ref[...].astype(BF16),
                           lam, sg_ref[...], lam_init)
    for h in range(A_HEADS):
        o_ref[:, 2 * A_HD * h:2 * A_HD * (h + 1)] = outs[h]
    qb = qb_ref[...].astype(BF16)
    kb = kb_ref[...].astype(BF16)
    vb = vb_ref[...].astype(BF16)
    for h in range(B_HEADS):
        sl = slice(B_HD * h, B_HD * (h + 1))
        p = _softmax_rows(_dot_nt(qb[:, sl], kb[:, sl]) * (B_HD ** -0.5))
        o_ref[:, A_W + B_HD * h:A_W + B_HD * (h + 1)] = _dot(p.astype(BF16), vb[:, sl])


def attn_ctx(proj, a_lam, subln_g, lam_init):
    specs = [pl.BlockSpec((SEQ, A_W), lambda b, c=c: (b, c)) for c in range(6)]
    return pl.pallas_call(
        functools.partial(_attn_ctx_kernel, lam_init=lam_init),
        grid=(BATCH,),
        in_specs=specs + [pl.BlockSpec((4, A_HD), lambda b: (0, 0)),
                          pl.BlockSpec((1, 2 * A_HD), lambda b: (0, 0))],
        out_specs=pl.BlockSpec((SEQ, A_W + B_W), lambda b: (b, 0)),
        out_shape=jax.ShapeDtypeStruct((N_CTX, A_W + B_W), F32),
        compiler_params=_cparams(1),
        name="attn_ctx",
    )(proj, proj, proj, proj, proj, proj, a_lam, subln_g.reshape(1, 2 * A_HD))


def _rope_tables():
    half = A_HD // 2
    t = np.arange(DEC_SEQ)
    pos = np.stack([t // GRID_W, t % GRID_W], axis=-1).astype(np.float32)
    freqs = (np.float32(ROPE_BASE) ** (-np.arange(0, half, 2, dtype=np.float32) / np.float32(half))).astype(np.float32)
    ang = (pos[:, :, None] * freqs).astype(np.float32)
    cos = np.cos(ang).astype(np.float32)
    sin = np.sin(ang).astype(np.float32)
    cos_h = np.concatenate([cos[:, 0], cos[:, 0], cos[:, 1], cos[:, 1]], axis=-1)
    sin_h = np.concatenate([-sin[:, 0], sin[:, 0], -sin[:, 1], sin[:, 1]], axis=-1)
    reps = A_W // A_HD
    return np.tile(cos_h, (1, reps)), np.tile(sin_h, (1, reps))


def _rope(x, cos, sin):
    quarter = A_HD // 4
    cols = []
    for c in range(x.shape[1] // LANES):
        xs = x[:, c * LANES:(c + 1) * LANES]
        up = pltpu.roll(xs, LANES - quarter, 1)
        dn = pltpu.roll(xs, quarter, 1)
        lane = lax.broadcasted_iota(jnp.int32, xs.shape, 1)
        cols.append(jnp.where((lane % (2 * quarter)) < quarter, up, dn))
    partner = jnp.concatenate(cols, axis=1)
    return x * cos + partner * sin


def _attn_lat_a_kernel(q_ref, k_ref, v_ref, ck_ref, cv_ref, cosq_ref, sinq_ref, cosk_ref, sink_ref,
                       lam_ref, sg_ref, o_ref, k_s, v_s, *, lam_init):
    @pl.when(pl.program_id(1) == 0)
    def _():
        k_s[0:DEC_SEQ, :] = _rope(k_ref[...], cosk_ref[...], sink_ref[...]).astype(BF16)
        k_s[DEC_SEQ:DEC_SEQ + PAST_LEN, :] = ck_ref[...].astype(BF16)
        v_s[0:DEC_SEQ, :] = v_ref[...].astype(BF16)
        v_s[DEC_SEQ:DEC_SEQ + PAST_LEN, :] = cv_ref[...].astype(BF16)

    lam = _lam_value(lam_ref, lam_init)
    q = _rope(q_ref[...], cosq_ref[...], sinq_ref[...]).astype(BF16)
    outs = _diff_attention(q, k_s[...], v_s[...], lam, sg_ref[...], lam_init)
    for h in range(A_HEADS):
        o_ref[:, 2 * A_HD * h:2 * A_HD * (h + 1)] = outs[h]


def attn_lat_a(proj, cache_k, cache_v, e, a_lam, subln_g, lam_init, *, tq):
    cos, sin = _rope_tables()
    nq = DEC_SEQ // tq
    q0 = N_CTX // tq
    b0 = N_CTX // DEC_SEQ
    cache_spec = pl.BlockSpec((None, None, PAST_LEN, A_W), lambda b, i: (b, e, 0, 0))
    return pl.pallas_call(
        functools.partial(_attn_lat_a_kernel, lam_init=lam_init),
        grid=(DEC_BATCH, nq),
        in_specs=[pl.BlockSpec((tq, A_W), lambda b, i: (q0 + b * nq + i, 0)),
                  pl.BlockSpec((DEC_SEQ, A_W), lambda b, i: (b0 + b, 1)),
                  pl.BlockSpec((DEC_SEQ, A_W), lambda b, i: (b0 + b, 2)),
                  cache_spec, cache_spec,
                  pl.BlockSpec((tq, A_W), lambda b, i: (i, 0)),
                  pl.BlockSpec((tq, A_W), lambda b, i: (i, 0)),
                  pl.BlockSpec((DEC_SEQ, A_W), lambda b, i: (0, 0)),
                  pl.BlockSpec((DEC_SEQ, A_W), lambda b, i: (0, 0)),
                  pl.BlockSpec((4, A_HD), lambda b, i: (0, 0)),
                  pl.BlockSpec((1, 2 * A_HD), lambda b, i: (0, 0))],
        out_specs=pl.BlockSpec((tq, A_W), lambda b, i: (b * nq + i, 0)),
        out_shape=jax.ShapeDtypeStruct((N_LAT, A_W), F32),
        scratch_shapes=[pltpu.VMEM((DEC_SEQ + PAST_LEN, A_W), BF16),
                        pltpu.VMEM((DEC_SEQ + PAST_LEN, A_W), BF16)],
        compiler_params=_cparams(2),
        name="attn_lat_a",
    )(proj, proj, proj, cache_k, cache_v, cos, sin, cos, sin, a_lam, subln_g.reshape(1, 2 * A_HD))


NA_WIN = NA_ROWS * GRID_W


def _na_bias_kernel(rpb_ref, rsel_ref, csel_ref, valid_ref, o_ref):
    rows = jnp.dot(rsel_ref[...], rpb_ref[0], precision=HI, preferred_element_type=F32)
    full = jnp.dot(rows, csel_ref[...], precision=HI, preferred_element_type=F32)
    o_ref[0] = jnp.where(valid_ref[...] > 0.5, full, NEG_INF)


def _na_bias(rpb):
    n_dr = 2 * NA_ROWS
    n_dc = 2 * NA_COLS
    r = np.arange(GRID_ROWS)
    r0 = np.clip(r - NA_ROWS // 2, 0, GRID_ROWS - NA_ROWS)
    krow = r0[:, None] + np.arange(NA_ROWS)
    dr = (krow - r[:, None] + NA_ROWS - 1).reshape(-1)
    qcol = np.arange(GRID_W)
    kcol = np.arange(GRID_W)
    cstart = np.clip(qcol - NA_COLS // 2, 0, GRID_W - NA_COLS)
    valid = (kcol[None, :] >= cstart[:, None]) & (kcol[None, :] < cstart[:, None] + NA_COLS)
    dc = np.clip(kcol[None, :] - qcol[:, None] + NA_COLS - 1, 0, 2 * NA_COLS - 2).reshape(-1)
    rsel = (dr[:, None] == np.arange(n_dr)[None, :]).astype(np.float32)
    csel = (np.arange(n_dc)[:, None] == dc[None, :]).astype(np.float32)
    rpb_p = jnp.pad(rpb.astype(F32), ((0, 0), (0, 1), (0, 1)))
    nrk = GRID_ROWS * NA_ROWS
    nqk = GRID_W * GRID_W
    full = lambda shape: pl.BlockSpec(shape, lambda h: (0,) * len(shape))
    bias = pl.pallas_call(
        _na_bias_kernel,
        grid=(B_HEADS,),
        in_specs=[pl.BlockSpec((1, n_dr, n_dc), lambda h: (h, 0, 0)),
                  full((nrk, n_dr)), full((n_dc, nqk)), full((1, nqk))],
        out_specs=pl.BlockSpec((1, nrk, nqk), lambda h: (h, 0, 0)),
        out_shape=jax.ShapeDtypeStruct((B_HEADS, nrk, nqk), F32),
        compiler_params=_cparams(1),
        name="na_bias",
    )(rpb_p, rsel, csel, valid.reshape(1, nqk).astype(np.float32))
    bias = bias.reshape(B_HEADS, GRID_ROWS, NA_ROWS, GRID_W, GRID_W).transpose(0, 1, 3, 2, 4)
    return bias.reshape(B_HEADS, GRID_ROWS, GRID_W, NA_WIN)


def _attn_lat_b_kernel(q_ref, k_ref, v_ref, ck_ref, cv_ref, bias_ref, o_ref):
    r = pl.program_id(1)
    r0 = jnp.clip(r - NA_ROWS // 2, 0, GRID_ROWS - NA_ROWS)
    win = pl.ds(pl.multiple_of(r0 * GRID_W, GRID_W), NA_WIN)
    q = q_ref[...].astype(BF16)
    kw = k_ref[win, :].astype(BF16)
    vw = v_ref[win, :].astype(BF16)
    kc = ck_ref[...].astype(BF16)
    vc = cv_ref[...].astype(BF16)
    scale = B_HD ** -0.5
    for h in range(B_HEADS):
        sl = slice(B_HD * h, B_HD * (h + 1))
        s_win = _dot_nt(q[:, sl], kw[:, sl]) * scale + bias_ref[h, 0]
        s_ctx = _dot_nt(q[:, sl], kc[:, sl]) * scale
        m = jnp.maximum(jnp.max(s_win, axis=-1, keepdims=True), jnp.max(s_ctx, axis=-1, keepdims=True))
        e_win = jnp.exp(s_win - m)
        e_ctx = jnp.exp(s_ctx - m)
        den = jnp.sum(e_win, axis=-1, keepdims=True) + jnp.sum(e_ctx, axis=-1, keepdims=True)
        p_win = (e_win / den).astype(BF16)
        p_ctx = (e_ctx / den).astype(BF16)
        o_ref[:, sl] = _dot(p_win, vw[:, sl]) + _dot(p_ctx, vc[:, sl])


def attn_lat_b(proj, cache_k, cache_v, e, rpb):
    bias = _na_bias(rpb)
    q0 = N_CTX // GRID_W
    b0 = N_CTX // DEC_SEQ
    cache_spec = pl.BlockSpec((None, None, PAST_LEN, B_W), lambda b, r: (b, e, 0, 0))
    return pl.pallas_call(
        _attn_lat_b_kernel,
        grid=(DEC_BATCH, GRID_ROWS),
        in_specs=[pl.BlockSpec((GRID_W, B_W), lambda b, r: (q0 + b * GRID_ROWS + r, 3)),
                  pl.BlockSpec((DEC_SEQ, B_W), lambda b, r: (b0 + b, 4)),
                  pl.BlockSpec((DEC_SEQ, B_W), lambda b, r: (b0 + b, 5)),
                  cache_spec, cache_spec,
                  pl.BlockSpec((B_HEADS, 1, GRID_W, NA_WIN), lambda b, r: (0, r, 0, 0))],
        out_specs=pl.BlockSpec((GRID_W, B_W), lambda b, r: (b * GRID_ROWS + r, 0)),
        out_shape=jax.ShapeDtypeStruct((N_LAT, B_W), F32),
        compiler_params=_cparams(2),
        name="attn_lat_b",
    )(proj, proj, proj, cache_k, cache_v, bias)


CONV_PAD = 8


def _conv_centred(x_ref, pad_ref, w_ref, seq):
    c = pad_ref.shape[1]
    pad_ref[0:CONV_PAD, :] = jnp.zeros((CONV_PAD, c), F32)
    pad_ref[CONV_PAD + seq:2 * CONV_PAD + seq, :] = jnp.zeros((CONV_PAD, c), F32)
    pad_ref[CONV_PAD:CONV_PAD + seq, :] = x_ref[...]
    left = (CONV_W - 1) // 2
    acc = None
    for j in range(CONV_W):
        start = CONV_PAD - left + j
        term = w_ref[j:j + 1, :] * pad_ref[start:start + seq, :]
        acc = term if acc is None else acc + term
    return acc


def _mlstm_kernel(qk_ref, v_ref, o_ref, g_ref, gt_ref, gb_ref, gbt_ref, cw_ref, ng_ref,
                  c0_ref, n0_ref, m0_ref, out_ref, cf_ref, nf_ref, mf_ref,
                  pad_s, qk_s, acc_s, c_s, n_s, m_s, *, seq):
    L = C_CHUNK
    nc = seq // L
    qk_s[...] = _silu(_conv_centred(qk_ref, pad_s, cw_ref, seq))
    row = lax.broadcasted_iota(jnp.int32, (L, L), 0)
    col = lax.broadcasted_iota(jnp.int32, (L, L), 1)
    gb = gb_ref[...]
    gbt = gbt_ref[...]
    ng = ng_ref[...]
    for d in range(2):
        tri = (col <= row) if d == 0 else (col >= row)
        tri_f = tri.astype(F32)
        tri_t = ((row <= col) if d == 0 else (row >= col)).astype(F32)
        last = L - 1 if d == 0 else 0
        for h in range(C_HEADS):
            c_s[h] = c0_ref[d * C_HEADS + h]
        n_s[...] = n0_ref[d * C_HEADS:(d + 1) * C_HEADS]
        m_s[...] = m0_ref[d * C_HEADS:(d + 1) * C_HEADS]

        def chunk_step(ci, carry, d=d, tri=tri, tri_f=tri_f, tri_t=tri_t, last=last):
            c_idx = ci if d == 0 else nc - 1 - ci
            rows = pl.ds(pl.multiple_of(c_idx * L, L), L)
            g = g_ref[rows, :] + gb
            gt = gt_ref[c_idx] + gbt
            ls = -_softplus(-g)
            lst = -_softplus(-gt)
            bcol = jnp.dot(tri_f, ls, precision=HI, preferred_element_type=F32)
            brow = jnp.dot(lst, tri_t, precision=HI, preferred_element_type=F32)
            prev = [(m_s[h][:, 0:1], n_s[h], c_s[h]) for h in range(C_HEADS)]
            new = []
            for h in range(C_HEADS):
                ci_ = d * 2 * C_HEADS + h
                cf_ = ci_ + C_HEADS
                b_c = bcol[:, cf_:cf_ + 1]
                b_r = brow[cf_:cf_ + 1, :]
                i_c = g[:, ci_:ci_ + 1]
                i_r = gt[ci_:ci_ + 1, :]
                m_prev, n_prev, c_prev = prev[h]
                dlog = jnp.where(tri, b_c - b_r + i_r, -jnp.inf)
                inter = b_c + m_prev
                m_t = jnp.maximum(inter, jnp.max(dlog, axis=-1, keepdims=True))
                w_in = jnp.exp(dlog - m_t)
                w_st = jnp.exp(inter - m_t)
                q = qk_s[rows, C_DK * h:C_DK * (h + 1)] * (C_DK ** -0.5)
                k = qk_s[rows, C_W + C_DK * h:C_W + C_DK * (h + 1)]
                v = v_ref[rows, C_DV * h:C_DV * (h + 1)]
                qb = q.astype(BF16)
                kb = k.astype(BF16)
                vb = v.astype(BF16)
                s = _dot_nt(qb, kb) * w_in
                num = w_st * _dot(qb, c_prev.astype(BF16)) + _dot(s.astype(BF16), vb)
                nq = w_st * jnp.sum(q * n_prev, axis=-1, keepdims=True) + jnp.sum(s, axis=-1, keepdims=True)
                hh = num / jnp.maximum(jnp.abs(nq), jnp.exp(-m_t))
                b_last = b_c[last:last + 1, :]
                wlog = b_last - b_c + i_c
                m_new = jnp.maximum(b_last + m_prev, jnp.max(wlog, axis=0, keepdims=True))
                ws = jnp.exp(wlog - m_new)
                wc = jnp.exp(b_last + m_prev - m_new)
                kw = ws * k
                new.append((jnp.broadcast_to(m_new, (1, LANES)),
                            wc * n_prev + jnp.sum(kw, axis=0, keepdims=True),
                            wc * c_prev + _dot_tn(kw.astype(BF16), vb), hh))
            for h in range(C_HEADS):
                m_s[h], n_s[h], c_s[h], hh = new[h]
                hs = slice(C_DV * h, C_DV * (h + 1))
                if d == 0:
                    acc_s[rows, hs] = hh
                else:
                    tot = acc_s[rows, hs] + hh
                    out_ref[rows, hs] = _rms(tot, ng) * _sigmoid(o_ref[rows, hs])
            return carry

        lax.fori_loop(0, nc, chunk_step, 0)
        for h in range(C_HEADS):
            cf_ref[d * C_HEADS + h] = c_s[h]
        nf_ref[d * C_HEADS:(d + 1) * C_HEADS] = n_s[...]
        mf_ref[d * C_HEADS:(d + 1) * C_HEADS] = m_s[...]


def mlstm(proj, gates_t, gate_b, conv_c, norm_g, c0, n0, m0, *, seq, nb, row0):
    nh = 2 * C_HEADS
    b0 = row0 // seq
    nc = seq // C_CHUNK
    gate_cols = (IN_ODD_PAD - LANES) // LANES
    gb = jnp.zeros((1, LANES), F32).at[0, :4 * C_HEADS].set(gate_b)
    out_shapes = (jax.ShapeDtypeStruct((nb * seq, C_W), F32),
                  jax.ShapeDtypeStruct((nb, nh, C_DK, C_DV), F32),
                  jax.ShapeDtypeStruct((nb, nh, 1, C_DK), F32),
                  jax.ShapeDtypeStruct((nb, nh, 1, LANES), F32))
    state_specs = [pl.BlockSpec((None, nh, C_DK, C_DV), lambda b: (b, 0, 0, 0)),
                   pl.BlockSpec((None, nh, 1, C_DK), lambda b: (b, 0, 0, 0)),
                   pl.BlockSpec((None, nh, 1, LANES), lambda b: (b, 0, 0, 0))]
    return pl.pallas_call(
        functools.partial(_mlstm_kernel, seq=seq),
        grid=(nb,),
        in_specs=[pl.BlockSpec((seq, 2 * C_W), lambda b: (b0 + b, 0)),
                  pl.BlockSpec((seq, C_W), lambda b: (b0 + b, 2)),
                  pl.BlockSpec((seq, C_W), lambda b: (b0 + b, 3)),
                  pl.BlockSpec((seq, LANES), lambda b: (b0 + b, gate_cols)),
                  pl.BlockSpec((nc, 4 * C_HEADS, C_CHUNK), lambda b: (b, 0, 0)),
                  pl.BlockSpec((1, LANES), lambda b: (0, 0)),
                  pl.BlockSpec((4 * C_HEADS, 1), lambda b: (0, 0)),
                  pl.BlockSpec((CONV_W, 2 * C_W), lambda b: (0, 0)),
                  pl.BlockSpec((1, C_DV), lambda b: (0, 0))] + state_specs,
        out_specs=[pl.BlockSpec((seq, C_W), lambda b: (b, 0))] + state_specs,
        out_shape=out_shapes,
        scratch_shapes=[pltpu.VMEM((seq + 2 * CONV_PAD, 2 * C_W), F32),
                        pltpu.VMEM((seq, 2 * C_W), F32),
                        pltpu.VMEM((seq, C_W), F32),
                        pltpu.VMEM((C_HEADS, C_DK, C_DV), F32),
                        pltpu.VMEM((C_HEADS, 1, C_DK), F32),
                        pltpu.VMEM((C_HEADS, 1, LANES), F32)],
        compiler_params=_cparams(1),
        name="mlstm",
    )(proj, proj, proj, proj, gates_t, gb, gate_b.reshape(4 * C_HEADS, 1), conv_c,
      norm_g.reshape(1, C_DV), c0, n0, m0)


RG_BLOCK = 64


def _rglru_kernel(xd_ref, gd_ref, cw_ref, cb_ref, wa_ref, wx_ref, ba_ref, bx_ref, lam_ref, h0_ref,
                  out_ref, hf_ref, pad_s, xc_s, acc_s, *, seq):
    R = RG_BLOCK
    nblk = seq // R
    xc_s[...] = _conv_centred(xd_ref, pad_s, cw_ref, seq) + cb_ref[...]
    row = lax.broadcasted_iota(jnp.int32, (R, D_RNN), 0)
    for d in range(2):
        sp = _softplus(-lam_ref[d])
        wa = wa_ref[d]
        wx = wx_ref[d]
        ba = ba_ref[d]
        bx = bx_ref[d]

        def block_step(bi, carry, d=d, sp=sp, wa=wa, wx=wx, ba=ba, bx=bx):
            b_idx = bi if d == 0 else nblk - 1 - bi
            rows = pl.ds(pl.multiple_of(b_idx * R, R), R)
            x = xc_s[rows, :]
            xb = x.astype(BF16)
            r = _sigmoid(_dot(xb, wa) + ba)
            ig = _sigmoid(_dot(xb, wx) + bx)
            log_a = -RG_C * r * sp
            a = jnp.exp(log_a)
            u = jnp.sqrt(-jnp.tanh(log_a) * (a * a + 1.0)) * (ig * x)
            sh = 1
            while sh < R:
                if d == 0:
                    a_s = pltpu.roll(a, sh, 0)
                    u_s = pltpu.roll(u, sh, 0)
                    ok = row >= sh
                else:
                    a_s = pltpu.roll(a, R - sh, 0)
                    u_s = pltpu.roll(u, R - sh, 0)
                    ok = row < R - sh
                u = u + a * jnp.where(ok, u_s, 0.0)
                a = a * jnp.where(ok, a_s, 1.0)
                sh *= 2
            hs = u + a * carry
            if d == 0:
                acc_s[rows, :] = hs
                return hs[R - 1:R, :]
            out_ref[rows, :] = (acc_s[rows, :] + hs) * _gelu_tanh(gd_ref[rows, :])
            return hs[0:1, :]

        hf_ref[d] = lax.fori_loop(0, nblk, block_step, h0_ref[d])


def _block_diag(w):
    eye = jnp.eye(D_BLOCKS, dtype=w.dtype)
    return (w[:, :, None, :] * eye[:, None, :, None]).reshape(D_RNN, D_RNN)


def rglru(proj, conv_d, conv_d_b, rg_wa, rg_ba, rg_wx, rg_bx, rg_lam, h0, *, seq, nb, row0):
    b0 = row0 // seq
    wa = jnp.stack([_block_diag(rg_wa[0]), _block_diag(rg_wa[1])]).astype(BF16)
    wx = jnp.stack([_block_diag(rg_wx[0]), _block_diag(rg_wx[1])]).astype(BF16)
    vec = lambda a: a.reshape(2, 1, D_RNN)
    full = lambda shape: pl.BlockSpec(shape, lambda b: (0,) * len(shape))
    return pl.pallas_call(
        functools.partial(_rglru_kernel, seq=seq),
        grid=(nb,),
        in_specs=[pl.BlockSpec((seq, D_RNN), lambda b: (b0 + b, 4)),
                  pl.BlockSpec((seq, D_RNN), lambda b: (b0 + b, 5)),
                  full((CONV_W, D_RNN)), full((1, D_RNN)),
                  full((2, D_RNN, D_RNN)), full((2, D_RNN, D_RNN)),
                  full((2, 1, D_RNN)), full((2, 1, D_RNN)), full((2, 1, D_RNN)),
                  pl.BlockSpec((None, 2, 1, D_RNN), lambda b: (b, 0, 0, 0))],
        out_specs=[pl.BlockSpec((seq, D_RNN), lambda b: (b, 0)),
                   pl.BlockSpec((None, 2, 1, D_RNN), lambda b: (b, 0, 0, 0))],
        out_shape=(jax.ShapeDtypeStruct((nb * seq, D_RNN), F32),
                   jax.ShapeDtypeStruct((nb, 2, 1, D_RNN), F32)),
        scratch_shapes=[pltpu.VMEM((seq + 2 * CONV_PAD, D_RNN), F32),
                        pltpu.VMEM((seq, D_RNN), F32),
                        pltpu.VMEM((seq, D_RNN), F32)],
        compiler_params=_cparams(1),
        name="rglru",
    )(proj, proj, conv_d, conv_d_b.reshape(1, D_RNN), wa, wx, vec(rg_ba), vec(rg_bx), vec(rg_lam), h0)


def _router_top2(h, rw_ref, rb_ref):
    logits = jnp.dot(h, rw_ref[...], precision=HI, preferred_element_type=F32) + rb_ref[...]
    lane = lax.broadcasted_iota(jnp.int32, logits.shape, 1)
    logits = jnp.where(lane < N_EXPERTS, logits, -jnp.inf)
    m1 = jnp.max(logits, axis=-1, keepdims=True)
    i1 = jnp.min(jnp.where(logits == m1, lane, LANES), axis=-1, keepdims=True)
    rest = jnp.where(lane == i1, -jnp.inf, logits)
    m2 = jnp.max(rest, axis=-1, keepdims=True)
    i2 = jnp.min(jnp.where(rest == m2, lane, LANES), axis=-1, keepdims=True)
    e2 = jnp.exp(m2 - m1)
    den = 1.0 + e2
    gates = jnp.where(lane == i1, 1.0 / den, 0.0) + jnp.where(lane == i2, e2 / den, 0.0)
    chosen = jnp.where((lane == i1) | (lane == i2), 1.0, 0.0)
    return gates, chosen


MOE_MAIN = 288
MOE_EXTRA = 128
ROUTED_MARK = 2048.0


def _moe_kernel(x_ref, sh_ref, sc_ref, gate_ref, gpre_ref, gpost_ref, rw_ref, rb_ref,
                w1_ref, w3_ref, w2_ref, o_ref, h_s, acc_s, dg_s, vcol_s, vrow_s, cnt_s, xs_s, y_s, *, tm):
    e = pl.program_id(1)
    f = pl.program_id(2)
    last_f = pl.num_programs(2) - 1

    @pl.when((e == 0) & (f == 0))
    def _():
        h = _norm_mod(x_ref[...], gpre_ref[...], sc_ref[0], sh_ref[0])
        h_s[...] = h.astype(BF16)
        acc_s[...] = jnp.zeros_like(acc_s)
        gates, chosen = _router_top2(h, rw_ref, rb_ref)
        dg_s[...] = gates
        cnt_s[...] = jnp.sum(chosen, axis=0, keepdims=True)
        r = lax.broadcasted_iota(jnp.int32, (tm, tm), 0)
        c = lax.broadcasted_iota(jnp.int32, (tm, tm), 1)
        low = jnp.where(c < r, 1.0, jnp.where(c == r, ROUTED_MARK, 0.0)).astype(BF16)
        upp = jnp.where(r < c, 1.0, jnp.where(c == r, ROUTED_MARK, 0.0)).astype(BF16)
        ch = chosen.astype(BF16)
        vcol_s[...] = _dot(low, ch)
        vrow_s[...] = _dot_tn(ch, upp)

    lane = lax.broadcasted_iota(jnp.int32, (tm, LANES), 1)
    lane1 = lax.broadcasted_iota(jnp.int32, (1, LANES), 1)
    cnt_e = jnp.sum(jnp.where(lane1 == e, cnt_s[...], 0.0))

    def gather_rows(r0, nr):
        row = vrow_s[pl.ds(e, 1), :]
        want = lax.broadcasted_iota(jnp.int32, (nr, tm), 0).astype(F32) + (r0 + ROUTED_MARK)
        p = jnp.where(want == row, 1.0, 0.0).astype(BF16)
        xs_s[r0:r0 + nr, :] = _dot(p, h_s[...]).astype(BF16)
        y_s[r0:r0 + nr, :] = jnp.zeros((nr, D_MODEL), F32)

    def expert_rows(r0, nr):
        xs = xs_s[r0:r0 + nr, :]
        u = _silu(_dot(xs, w1_ref[0])) * _dot(xs, w3_ref[0])
        y_s[r0:r0 + nr, :] += _dot(u.astype(BF16), w2_ref[0])

    def scatter_rows(r0, nr):
        col = jnp.sum(jnp.where(lane == e, vcol_s[...], 0.0), axis=-1, keepdims=True)
        g = jnp.sum(jnp.where(lane == e, dg_s[...], 0.0), axis=-1, keepdims=True)
        want = lax.broadcasted_iota(jnp.int32, (tm, nr), 1).astype(F32) + (r0 + ROUTED_MARK)
        q = jnp.where(want == col, 1.0, 0.0).astype(BF16)
        acc_s[...] += g * _dot(q, y_s[r0:r0 + nr, :].astype(BF16))

    blocks = [(0, MOE_MAIN, None)]
    r0 = MOE_MAIN
    while r0 < tm:
        blocks.append((r0, MOE_EXTRA, cnt_e > r0))
        r0 += MOE_EXTRA
    for r0, nr, needed in blocks:
        first = (f == 0) if needed is None else ((f == 0) & needed)
        final = (f == last_f) if needed is None else ((f == last_f) & needed)
        pl.when(first)(functools.partial(gather_rows, r0, nr))
        if needed is None:
            expert_rows(r0, nr)
        else:
            pl.when(needed)(functools.partial(expert_rows, r0, nr))
        pl.when(final)(functools.partial(scatter_rows, r0, nr))

    @pl.when((e == pl.num_programs(1) - 1) & (f == last_f))
    def _():
        o_ref[...] = x_ref[...] + gate_ref[0] * _rms(acc_s[...], gpost_ref[...])


def moe_mixer(x, mod, g_pre, g_post, w1, w3, w2, router_w, router_b, *, tm, tf):
    n_e, _, ff = w1.shape
    n_rows = MOE_MAIN + MOE_EXTRA * (-(-(tm - MOE_MAIN) // MOE_EXTRA))
    const = lambda shape: pl.BlockSpec(shape, lambda i, e, f: (0,) * len(shape))
    return pl.pallas_call(
        functools.partial(_moe_kernel, tm=tm),
        grid=(N_TOK // tm, n_e, ff // tf),
        in_specs=[pl.BlockSpec((tm, D_MODEL), lambda i, e, f: (i, 0)),
                  _mod_spec(3, tm), _mod_spec(4, tm), _mod_spec(5, tm),
                  const((1, D_MODEL)), const((1, D_MODEL)),
                  const((D_MODEL, LANES)), const((1, LANES)),
                  pl.BlockSpec((1, D_MODEL, tf), lambda i, e, f: (e, 0, f)),
                  pl.BlockSpec((1, D_MODEL, tf), lambda i, e, f: (e, 0, f)),
                  pl.BlockSpec((1, tf, D_MODEL), lambda i, e, f: (e, f, 0))],
        out_specs=pl.BlockSpec((tm, D_MODEL), lambda i, e, f: (i, 0)),
        out_shape=jax.ShapeDtypeStruct((N_TOK, D_MODEL), F32),
        scratch_shapes=[pltpu.VMEM((tm, D_MODEL), BF16),
                        pltpu.VMEM((tm, D_MODEL), F32),
                        pltpu.VMEM((tm, LANES), F32),
                        pltpu.VMEM((tm, LANES), F32),
                        pltpu.VMEM((LANES, tm), F32),
                        pltpu.VMEM((1, LANES), F32),
                        pltpu.VMEM((n_rows, D_MODEL), BF16),
                        pltpu.VMEM((n_rows, D_MODEL), F32)],
        compiler_params=_cparams(3),
        name="moe_routed",
    )(x, mod, mod, mod, g_pre.reshape(1, D_MODEL), g_post.reshape(1, D_MODEL), router_w, router_b, w1, w3, w2)


def _swiglu_kernel(x_ref, sh_ref, sc_ref, gate_ref, gpre_ref, gpost_ref, w1_ref, w3_ref, w2_ref, o_ref,
                   h_s, acc_s):
    f = pl.program_id(1)

    @pl.when(f == 0)
    def _():
        h_s[...] = _norm_mod(x_ref[...], gpre_ref[...], sc_ref[0], sh_ref[0]).astype(BF16)
        acc_s[...] = jnp.zeros_like(acc_s)

    h = h_s[...]
    u = _silu(_dot(h, w1_ref[...].astype(BF16))) * _dot(h, w3_ref[...].astype(BF16))
    acc_s[...] += _dot(u.astype(BF16), w2_ref[...].astype(BF16))

    @pl.when(f == pl.num_programs(1) - 1)
    def _():
        o_ref[...] = x_ref[...] + gate_ref[0] * _rms(acc_s[...], gpost_ref[...])


def swiglu_mixer(x, mod, g_pre, g_post, w1, w3, w2, *, tm, tf):
    ff = w1.shape[1]
    const = lambda shape: pl.BlockSpec(shape, lambda i, f: (0,) * len(shape))
    return pl.pallas_call(
        _swiglu_kernel,
        grid=(N_TOK // tm, ff // tf),
        in_specs=[pl.BlockSpec((tm, D_MODEL), lambda i, f: (i, 0)),
                  _mod_spec(3, tm), _mod_spec(4, tm), _mod_spec(5, tm),
                  const((1, D_MODEL)), const((1, D_MODEL)),
                  pl.BlockSpec((D_MODEL, tf), lambda i, f: (0, f)),
                  pl.BlockSpec((D_MODEL, tf), lambda i, f: (0, f)),
                  pl.BlockSpec((tf, D_MODEL), lambda i, f: (f, 0))],
        out_specs=pl.BlockSpec((tm, D_MODEL), lambda i, f: (i, 0)),
        out_shape=jax.ShapeDtypeStruct((N_TOK, D_MODEL), F32),
        scratch_shapes=[pltpu.VMEM((tm, D_MODEL), BF16),
                        pltpu.VMEM((tm, D_MODEL), F32)],
        compiler_params=_cparams(2),
        name="swiglu_dense",
    )(x, mod, mod, mod, g_pre.reshape(1, D_MODEL), g_post.reshape(1, D_MODEL), w1, w3, w2)


def _even_layer(x, mod, g, e, lam_init, w_in, w_out, a_lam, subln_g, rpb, cache_a_k, cache_a_v,
                cache_b_k, cache_b_v, ff_w1, ff_w3, ff_w2):
    proj = nm_matmul(x, mod, g[0], w_in, tm=1024, tn=768)
    o_ctx = attn_ctx(proj, a_lam, subln_g, lam_init)
    o_a = attn_lat_a(proj, cache_a_k, cache_a_v, e, a_lam, subln_g, lam_init, tq=256)
    o_b = attn_lat_b(proj, cache_b_k, cache_b_v, e, rpb)
    x = out_residual((o_ctx, 0), (o_ctx, 1), (o_a, 0), (o_b, 0), w_out, x, mod, g[1], tm=512)
    x = swiglu_mixer(x, mod, g[2], g[3], ff_w1, ff_w3, ff_w2, tm=1024, tf=256)
    cols = lambda c0, c1: lax.slice(proj, (0, c0), (N_CTX, c1))
    new_kv = (cols(A_W, 2 * A_W).reshape(BATCH, SEQ, A_HEADS, 2 * A_HD),
              cols(2 * A_W, 3 * A_W).reshape(BATCH, SEQ, A_HEADS, 2 * A_HD),
              cols(3 * A_W + B_W, 3 * A_W + 2 * B_W).reshape(BATCH, SEQ, B_HEADS, B_HD),
              cols(3 * A_W + 2 * B_W, IN_EVEN).reshape(BATCH, SEQ, B_HEADS, B_HD))
    return x, new_kv


def _odd_w_in(w):
    g0 = 4 * C_W
    g1 = g0 + 4 * C_HEADS
    pad = jnp.zeros((D_MODEL, LANES - 4 * C_HEADS), w.dtype)
    return jnp.concatenate([w[:, :g0], w[:, g1:], w[:, g0:g1], pad], axis=1)


def _odd_layer(x, mod, g, j, w_in, gate_b, conv_c, conv_d, conv_d_b, rg_wa, rg_ba, rg_wx, rg_bx, rg_lam,
               c_norm_g, w_out, state_c_C, state_c_n, state_c_m, state_d_h,
               router_w, router_b, moe_w1, moe_w3, moe_w2):
    proj = nm_matmul(x, mod, g[0], _odd_w_in(w_in), tm=1024, tn=640)
    gates = proj[:, IN_ODD_PAD - LANES:IN_ODD_PAD - LANES + 4 * C_HEADS]
    gates_t = gates.reshape(N_TOK // C_CHUNK, C_CHUNK, 4 * C_HEADS).transpose(0, 2, 1)
    nh = 2 * C_HEADS
    nc_ctx = N_CTX // C_CHUNK
    zc = (jnp.zeros((BATCH, nh, C_DK, C_DV), F32), jnp.zeros((BATCH, nh, 1, C_DK), F32),
          jnp.zeros((BATCH, nh, 1, LANES), F32))
    sc = (state_c_C.reshape(DEC_BATCH, nh, C_DK, C_DV), state_c_n.reshape(DEC_BATCH, nh, 1, C_DK),
          jnp.broadcast_to(state_c_m.reshape(DEC_BATCH, nh, 1, 1), (DEC_BATCH, nh, 1, LANES)))
    hc_p, cf, nf, mf = mlstm(proj, gates_t[:nc_ctx], gate_b, conv_c, c_norm_g, *zc,
                             seq=SEQ, nb=BATCH, row0=0)
    hc_s, _, _, _ = mlstm(proj, gates_t[nc_ctx:], gate_b, conv_c, c_norm_g, *sc,
                          seq=DEC_SEQ, nb=DEC_BATCH, row0=N_CTX)
    rg = (conv_d, conv_d_b, rg_wa, rg_ba, rg_wx, rg_bx, rg_lam)
    hd_p, hfin = rglru(proj, *rg, jnp.zeros((BATCH, 2, 1, D_RNN), F32), seq=SEQ, nb=BATCH, row0=0)
    hd_s, _ = rglru(proj, *rg, state_d_h.reshape(DEC_BATCH, 2, 1, D_RNN),
                    seq=DEC_SEQ, nb=DEC_BATCH, row0=N_CTX)
    x = out_residual((hc_p, 0), (hd_p, 0), (hc_s, 0), (hd_s, 0), w_out, x, mod, g[1], tm=512)
    rw = jnp.zeros((D_MODEL, LANES), F32).at[:, :N_EXPERTS].set(router_w)
    rb = jnp.zeros((1, LANES), F32).at[0, :N_EXPERTS].set(router_b)
    x = moe_mixer(x, mod, g[2], g[3], moe_w1.astype(BF16), moe_w3.astype(BF16), moe_w2.astype(BF16),
                  rw, rb, tm=1024, tf=1024)
    states = (cf.reshape(BATCH, 2, C_HEADS, C_DK, C_DV), nf.reshape(BATCH, 2, C_HEADS, C_DK),
              mf[..., 0].reshape(BATCH, 2, C_HEADS), hfin.reshape(BATCH, 2, D_RNN))
    return x, states


def kernel(x_prompt, x_sample, c, cache_a_k, cache_a_v, cache_b_k, cache_b_v, state_c_C, state_c_n, state_c_m, state_d_h, c_ctx, mod_w, mod_b, norm_g, w_in_even, w_out_even, a_lam, a_subln_g, b_rpb, ff_w1, ff_w3, ff_w2, w_in_odd, c_gate_b, conv_c, conv_d, conv_d_b, rg_wa, rg_ba, rg_wx, rg_bx, rg_lam, c_norm_g, w_out_odd, router_w, router_b, moe_w1, moe_w3, moe_w2):
    x = jnp.concatenate([x_prompt.reshape(N_CTX, D_MODEL), x_sample.reshape(N_LAT, D_MODEL)], axis=0)
    cvec = jnp.concatenate([c_ctx[None], c, jnp.zeros((MOD_ROWS - 1 - DEC_BATCH, D_MODEL), F32)], axis=0)
    mods = adaln_all(cvec, mod_w, mod_b)
    ca_k = cache_a_k.reshape(DEC_BATCH, N_EVEN, PAST_LEN, A_W)
    ca_v = cache_a_v.reshape(DEC_BATCH, N_EVEN, PAST_LEN, A_W)
    cb_k = cache_b_k.reshape(DEC_BATCH, N_EVEN, PAST_LEN, B_W)
    cb_v = cache_b_v.reshape(DEC_BATCH, N_EVEN, PAST_LEN, B_W)
    new_kv = []
    new_st = []
    for l in range(DEPTH):
        if l % 2 == 0:
            e = l // 2
            lam_init = 0.8 - 0.6 * math.exp(-0.3 * l)
            x, kv = _even_layer(x, mods[l], norm_g[l], e, lam_init, w_in_even[e], w_out_even[e], a_lam[e],
                                a_subln_g[e], b_rpb[e], ca_k, ca_v, cb_k, cb_v, ff_w1[e], ff_w3[e], ff_w2[e])
            new_kv.append(kv)
        else:
            j = l // 2
            x, st = _odd_layer(x, mods[l], norm_g[l], j, w_in_odd[j], c_gate_b[j], conv_c[j], conv_d[j],
                               conv_d_b[j], rg_wa[j], rg_ba[j], rg_wx[j], rg_bx[j], rg_lam[j], c_norm_g[j],
                               w_out_odd[j], state_c_C[:, j], state_c_n[:, j], state_c_m[:, j], state_d_h[:, j],
                               router_w[j], router_b[j], moe_w1[j], moe_w3[j], moe_w2[j])
            new_st.append(st)
    y_p = x[:N_CTX].reshape(BATCH, SEQ, D_MODEL)
    y_s = x[N_CTX:].reshape(DEC_BATCH, DEC_SEQ, D_MODEL)
    kv_out = tuple(jnp.stack([kv[i] for kv in new_kv], axis=1) for i in range(4))
    st_out = tuple(jnp.stack([st[i] for st in new_st], axis=1) for i in range(4))
    return (y_p, y_s) + kv_out + st_out
```
